```python
import math
import jax, jax.numpy as jnp
from jax import lax
import numpy as np

D_MODEL = 2048
BATCH = 2
SEQ = 8192
DEPTH = 4

GRID_W = 64
CTX_LEN = 256
N_MIXERS = 4
D_FF = -(-8 * D_MODEL // (3 * 256)) * 256
D_RNN = -(-4 * D_MODEL // (3 * 256)) * 256
RG_BLOCKS = 16
RG_BLOCK = D_RNN // RG_BLOCKS
RG_CONV_W = 4
RG_C = 8.0
POOL_WINDOWS = (2, 4, 8, 16)
POOL_GROUPS = len(POOL_WINDOWS)
POOL_GROUP_DIM = D_MODEL // POOL_GROUPS
CONF_KERNEL = 31
FT_GROUPS = 4
FT_GROUP_DIM = D_MODEL // FT_GROUPS
ALPHA = (2 * DEPTH) ** 0.25
BETA = (8 * DEPTH) ** -0.25
LN_EPS = 1e-5
POS_THETA = 10000.0

kernel_name = "hybrid_interleaved_diffusion_trunk"


def _layer_norm(x, g, b):
    xf = x.astype(jnp.float32)
    mu = jnp.mean(xf, axis=-1, keepdims=True)
    xc = xf - mu
    var = jnp.mean(xc * xc, axis=-1, keepdims=True)
    y = xc * lax.rsqrt(var + LN_EPS) * g.astype(jnp.float32) + b.astype(jnp.float32)
    return y.astype(x.dtype)


def _pos_embed_2d(rows, cols, dim):
    quarter = dim // 4
    omega = 1.0 / (POS_THETA ** (jnp.arange(quarter, dtype=jnp.float32) / quarter))
    ar = jnp.arange(rows, dtype=jnp.float32)[:, None] * omega[None]
    ac = jnp.arange(cols, dtype=jnp.float32)[:, None] * omega[None]
    er = jnp.concatenate([jnp.sin(ar), jnp.cos(ar)], axis=-1)
    ec = jnp.concatenate([jnp.sin(ac), jnp.cos(ac)], axis=-1)
    pe = jnp.concatenate([jnp.broadcast_to(er[:, None, :], (rows, cols, dim // 2)),
                          jnp.broadcast_to(ec[None, :, :], (rows, cols, dim // 2))], axis=-1)
    return pe.reshape(rows * cols, dim)


def _depthwise_conv(x, w, b, pad_lo, pad_hi):
    ch = x.shape[-1]
    y = lax.conv_general_dilated(x, w[:, None, :].astype(x.dtype), window_strides=(1,),
                                 padding=[(pad_lo, pad_hi)],
                                 dimension_numbers=("NWC", "WIO", "NWC"),
                                 feature_group_count=ch)
    return y + b.astype(x.dtype)


def _linear_scan(a, b, h0, reverse):
    if h0 is not None:
        idx = -1 if reverse else 0
        b = b.at[:, idx].add(a[:, idx] * h0)

    def combine(l, r):
        return l[0] * r[0], r[0] * l[1] + r[1]

    _, h = lax.associative_scan(combine, (a, b), reverse=reverse, axis=1)
    return h


def _rglru_coeffs(xb, wr, br, wi, bi, lam):
    xf = xb.astype(jnp.float32)
    bsz, s, r_dim = xf.shape
    xblk = xf.reshape(bsz, s, RG_BLOCKS, RG_BLOCK)
    r = jax.nn.sigmoid(jnp.einsum("bsnk,nkj->bsnj", xblk, wr.astype(jnp.float32)).reshape(bsz, s, r_dim)
                       + br.astype(jnp.float32))
    i = jax.nn.sigmoid(jnp.einsum("bsnk,nkj->bsnj", xblk, wi.astype(jnp.float32)).reshape(bsz, s, r_dim)
                       + bi.astype(jnp.float32))
    log_a = -RG_C * r * jax.nn.softplus(-lam.astype(jnp.float32))
    a = jnp.exp(log_a)
    bcoef = jnp.sqrt(-jnp.expm1(2.0 * log_a)) * (i * xf)
    return a, bcoef


def _rglru_mixer(h, hc, w_gate, w_x, conv_w, conv_b, wr, br, wi, bi, lam, w_out, need_ctx_out):
    pad_lo, pad_hi = (RG_CONV_W - 1) // 2, RG_CONV_W // 2
    xb = _depthwise_conv(h @ w_x, conv_w, conv_b, pad_lo, pad_hi)
    xbc = _depthwise_conv(hc @ w_x, conv_w, conv_b, pad_lo, pad_hi)
    ys, ycs = [], []
    for d in range(2):
        rev = d == 1
        a_c, b_c = _rglru_coeffs(xbc, wr[d], br[d], wi[d], bi[d], lam[d])
        hs_c = _linear_scan(a_c, b_c, None, rev)
        h0 = hs_c[:, 0] if rev else hs_c[:, -1]
        a_l, b_l = _rglru_coeffs(xb, wr[d], br[d], wi[d], bi[d], lam[d])
        ys.append(_linear_scan(a_l, b_l, h0, rev))
        ycs.append(hs_c)
    y = (ys[0] + ys[1]).astype(h.dtype)
    out = (y * jax.nn.gelu(h @ w_gate)) @ w_out
    out_c = None
    if need_ctx_out:
        yc = (ycs[0] + ycs[1]).astype(hc.dtype)
        out_c = (yc * jax.nn.gelu(hc @ w_gate)) @ w_out
    return out, out_c


def _pool_mixer(h, w, b, scale):
    bsz, s, d = h.shape
    hf = h.astype(jnp.float32)
    cs = jnp.concatenate([jnp.zeros((bsz, 1, d), jnp.float32), lax.cumsum(hf, axis=1)], axis=1)
    t = jnp.arange(s)
    outs = []
    for g, win in enumerate(POOL_WINDOWS):
        sl = slice(g * POOL_GROUP_DIM, (g + 1) * POOL_GROUP_DIM)
        lo = jnp.clip(t - win // 2, 0, s)
        hi = jnp.clip(t - win // 2 + win, 0, s)
        csg = cs[:, :, sl]
        mean = (jnp.take(csg, hi, axis=1) - jnp.take(csg, lo, axis=1)) / (hi - lo).astype(jnp.float32)[:, None]
        outs.append(mean - hf[:, :, sl])
    p = jnp.stack(outs, axis=2)
    y = jnp.einsum("bsgk,gkj->bsgj", p, w.astype(jnp.float32)).reshape(bsz, s, d) + b.astype(jnp.float32)
    return (y * scale.astype(jnp.float32)).astype(h.dtype)


def _conformer_conv(h, w1, b1, dw, dwb, g, bb, w2, b2):
    d = h.shape[-1]
    u = h @ w1 + b1
    u = u[..., :d] * jax.nn.sigmoid(u[..., d:])
    pad = (CONF_KERNEL - 1) // 2
    u = _depthwise_conv(u, dw, dwb, pad, pad)
    u = jax.nn.silu(_layer_norm(u, g, bb))
    return u @ w2 + b2


def _fourier_mixer(h, w, b):
    bsz, s, d = h.shape
    f = jnp.fft.fftn(h.astype(jnp.float32).reshape(bsz, s, FT_GROUPS, FT_GROUP_DIM), axes=(1, 3), norm="ortho").real
    return f.reshape(bsz, s, d).astype(h.dtype) @ w + b


def _swiglu(h, w1, w3, w2):
    return (jax.nn.silu(h @ w1) * (h @ w3)) @ w2


def setup_inputs(seed: int = 0) -> dict:
    key = jax.random.key(seed)
    ks = iter(jax.random.split(key, 64))
    f32 = jnp.float32

    def nrm(shape, s):
        return jax.random.normal(next(ks), shape, f32) * s

    d = D_MODEL
    n_a = len(range(0, DEPTH, N_MIXERS))
    n_b = len(range(1, DEPTH, N_MIXERS))
    n_c = len(range(2, DEPTH, N_MIXERS))
    n_d = len(range(3, DEPTH, N_MIXERS))
    u = jax.random.uniform(next(ks), (n_a, 2, D_RNN), f32, minval=0.9, maxval=0.999)
    sig = u ** (1.0 / RG_C)
    rg_lam = jnp.log(sig) - jnp.log1p(-sig)
    return {
        "x": nrm((BATCH, SEQ, d), 1.0),
        "c": nrm((BATCH, d), 1.0),
        "ctx": nrm((BATCH, CTX_LEN, d), 1.0),
        "c_ctx": nrm((d,), 1.0),
        "mod_w": nrm((DEPTH, d, 6 * d), d ** -0.5),
        "mod_b": nrm((DEPTH, 6 * d), 0.02),
        "ln_g": 1.0 + nrm((DEPTH, 2, d), 0.02),
        "ln_b": nrm((DEPTH, 2, d), 0.02),
        "ffn_w1": nrm((DEPTH, d, D_FF), d ** -0.5),
        "ffn_w3": nrm((DEPTH, d, D_FF), d ** -0.5),
        "ffn_w2": nrm((DEPTH, D_FF, d), BETA * D_FF ** -0.5),
        "rg_w_gate": nrm((n_a, d, D_RNN), d ** -0.5),
        "rg_w_x": nrm((n_a, d, D_RNN), d ** -0.5),
        "rg_conv_w": nrm((n_a, RG_CONV_W, D_RNN), RG_CONV_W ** -0.5),
        "rg_conv_b": nrm((n_a, D_RNN), 0.02),
        "rg_wr": nrm((n_a, 2, RG_BLOCKS, RG_BLOCK, RG_BLOCK), RG_BLOCK ** -0.5),
        "rg_br": nrm((n_a, 2, D_RNN), 0.02),
        "rg_wi": nrm((n_a, 2, RG_BLOCKS, RG_BLOCK, RG_BLOCK), RG_BLOCK ** -0.5),
        "rg_bi": nrm((n_a, 2, D_RNN), 0.02),
        "rg_lam": rg_lam,
        "rg_w_out": nrm((n_a, D_RNN, d), BETA * D_RNN ** -0.5),
        "pool_w": nrm((n_b, POOL_GROUPS, POOL_GROUP_DIM, POOL_GROUP_DIM), BETA * POOL_GROUP_DIM ** -0.5),
        "pool_b": nrm((n_b, d), 0.02),
        "pool_scale": 1.0 + nrm((n_b, d), 0.1),
        "cv_w1": nrm((n_c, d, 2 * d), d ** -0.5),
        "cv_b1": nrm((n_c, 2 * d), 0.02),
        "cv_dw": nrm((n_c, CONF_KERNEL, d), CONF_KERNEL ** -0.5),
        "cv_dwb": nrm((n_c, d), 0.02),
        "cv_ln_g": 1.0 + nrm((n_c, d), 0.02),
        "cv_ln_b": nrm((n_c, d), 0.02),
        "cv_w2": nrm((n_c, d, d), BETA * d ** -0.5),
        "cv_b2": nrm((n_c, d), 0.02),
        "ft_w": nrm((n_d, d, d), BETA * d ** -0.5),
        "ft_b": nrm((n_d, d), 0.02),
    }


def reference(x, c, ctx, c_ctx, mod_w, mod_b, ln_g, ln_b, ffn_w1, ffn_w3, ffn_w2,
              rg_w_gate, rg_w_x, rg_conv_w, rg_conv_b, rg_wr, rg_br, rg_wi, rg_bi, rg_lam, rg_w_out,
              pool_w, pool_b, pool_scale,
              cv_w1, cv_b1, cv_dw, cv_dwb, cv_ln_g, cv_ln_b, cv_w2, cv_b2,
              ft_w, ft_b):
    bsz, s, d = x.shape
    rows = s // GRID_W
    x = x + _pos_embed_2d(rows, GRID_W, d).astype(x.dtype)[None]
    reader_layers = [i for i in range(DEPTH) if i % N_MIXERS == 0]
    last_reader = max(reader_layers) if reader_layers else -1
    xc = ctx
    cond_lat = jax.nn.silu(c)
    cond_ctx = jax.nn.silu(c_ctx)[None]
    for i in range(DEPTH):
        kind = i % N_MIXERS
        j = i // N_MIXERS
        ctx_out = i < last_reader
        m = jnp.split((cond_lat @ mod_w[i] + mod_b[i])[:, None, :], 6, axis=-1)
        h = x * (1 + m[1]) + m[0]
        hc, mc, yc = None, None, None
        if i <= last_reader:
            mc = jnp.split((cond_ctx @ mod_w[i] + mod_b[i])[:, None, :], 6, axis=-1)
            hc = xc * (1 + mc[1]) + mc[0]
        if kind == 0:
            y, yc = _rglru_mixer(h, hc, rg_w_gate[j], rg_w_x[j], rg_conv_w[j], rg_conv_b[j],
                                 rg_wr[j], rg_br[j], rg_wi[j], rg_bi[j], rg_lam[j], rg_w_out[j], ctx_out)
        else:
            if kind == 1:
                mix = lambda t: _pool_mixer(t, pool_w[j], pool_b[j], pool_scale[j])
            elif kind == 2:
                mix = lambda t: _conformer_conv(t, cv_w1[j], cv_b1[j], cv_dw[j], cv_dwb[j],
                                                cv_ln_g[j], cv_ln_b[j], cv_w2[j], cv_b2[j])
            else:
                mix = lambda t: _fourier_mixer(t, ft_w[j], ft_b[j])
            y = mix(h)
            if ctx_out:
                yc = mix(hc)
        x = _layer_norm(ALPHA * x + m[2] * y, ln_g[i, 0], ln_b[i, 0])
        h = x * (1 + m[4]) + m[3]
        x = _layer_norm(ALPHA * x + m[5] * _swiglu(h, ffn_w1[i], ffn_w3[i], ffn_w2[i]), ln_g[i, 1], ln_b[i, 1])
        if ctx_out:
            xc = _layer_norm(ALPHA * xc + mc[2] * yc, ln_g[i, 0], ln_b[i, 0])
            hc = xc * (1 + mc[4]) + mc[3]
            xc = _layer_norm(ALPHA * xc + mc[5] * _swiglu(hc, ffn_w1[i], ffn_w3[i], ffn_w2[i]), ln_g[i, 1], ln_b[i, 1])
    return x
```

```python
import functools

import numpy as np
import jax
import jax.numpy as jnp
from jax import lax
from jax.experimental import pallas as pl
from jax.experimental.pallas import tpu as pltpu

_F32 = jnp.float32
_BF16 = jnp.bfloat16

GRID_W = 64
N_MIXERS = 4
RG_BLOCKS = 16
RG_CONV_W = 4
RG_C = 8.0
POOL_WINDOWS = (2, 4, 8, 16)
CONF_KERNEL = 31
FT_GROUPS = 4
LN_EPS = 1e-5
POS_THETA = 10000.0

_V7X_LANES = 128
_V7X_SUBLANES = 8
_V7X_VMEM_BYTES = 64 * 1024 * 1024
_VMEM_LIMIT = _V7X_VMEM_BYTES - 12 * 1024 * 1024

_RG_BLOCK_PAD = 192
_RG_PAIR = 2 * _RG_BLOCK_PAD

_MOD_ROWS = 8


def _params(*sem):
    return pltpu.CompilerParams(dimension_semantics=sem, vmem_limit_bytes=_VMEM_LIMIT)


def _layer_norm(v, g, b):
    mu = jnp.mean(v, axis=-1, keepdims=True)
    xc = v - mu
    var = jnp.mean(xc * xc, axis=-1, keepdims=True)
    return xc * lax.rsqrt(var + LN_EPS) * g + b


def _dot(a, b):
    return jnp.dot(a, b, preferred_element_type=_F32)


def _cond_body(c_ref, w_ref, b_ref, o_ref):
    cond = c_ref[...]
    cond = cond * jax.nn.sigmoid(cond)
    o_ref[...] = _dot(cond.astype(_BF16), w_ref[...].astype(_BF16)) + b_ref[...]


def _cond_vectors(cond, mod_w, mod_b):
    depth, d, n = mod_w.shape
    tn = 1024
    return pl.pallas_call(
        _cond_body,
        grid=(depth, n // tn),
        in_specs=[pl.BlockSpec((_MOD_ROWS, d), lambda l, j: (0, 0)),
                  pl.BlockSpec((None, d, tn), lambda l, j: (l, 0, j)),
                  pl.BlockSpec((None, 1, tn), lambda l, j: (l, 0, j))],
        out_specs=pl.BlockSpec((None, _MOD_ROWS, tn), lambda l, j: (l, 0, j)),
        out_shape=jax.ShapeDtypeStruct((depth, _MOD_ROWS, n), _F32),
        compiler_params=_params("parallel", "parallel"),
        name="cond_vectors",
    )(cond, mod_w, mod_b.reshape(depth, 1, n))


def _rg_in_body(*refs, latent, tm):
    if latent:
        x_ref, er_ref, ec_ref, m_ref, wx_ref, wg_ref, x0_ref, xw_ref, gate_ref, h_scr = refs
    else:
        x_ref, m_ref, wx_ref, xw_ref, h_scr = refs

    @pl.when(pl.program_id(2) == 0)
    def _():
        if latent:
            half = er_ref.shape[1]
            for r in range(tm // GRID_W):
                rows = slice(r * GRID_W, (r + 1) * GRID_W)
                x0_ref[rows, :half] = x_ref[rows, :half] + er_ref[r:r + 1, :]
                x0_ref[rows, half:] = x_ref[rows, half:] + ec_ref[...]
            x0 = x0_ref[...]
        else:
            x0 = x_ref[...]
        h_scr[...] = (x0 * (1.0 + m_ref[1:2, :]) + m_ref[0:1, :]).astype(_BF16)

    h = h_scr[...]
    xw_ref[...] = _dot(h, wx_ref[...])
    if latent:
        gate_ref[...] = jax.nn.gelu(_dot(h, wg_ref[...])).astype(_BF16)


def _rg_in(x, mods, mod_row, w_x, w_gate, pos_tables):
    bsz, s, d = x.shape
    r = w_x.shape[1]
    latent = pos_tables is not None
    tm = 512 if latent else s
    tn = 512
    grid = (bsz, s // tm, r // tn)
    xspec = pl.BlockSpec((None, tm, d), lambda b, i, j: (b, i, 0))
    mspec = pl.BlockSpec((None, 6, d), lambda b, i, j: (mod_row(b), 0, 0))
    wspec = pl.BlockSpec((d, tn), lambda b, i, j: (0, j))
    ospec = pl.BlockSpec((None, tm, tn), lambda b, i, j: (b, i, j))
    body = functools.partial(_rg_in_body, latent=latent, tm=tm)
    scratch = [pltpu.VMEM((tm, d), _BF16)]
    if latent:
        er, ec = pos_tables
        rows_per_tile = tm // GRID_W
        return pl.pallas_call(
            body, grid=grid,
            in_specs=[xspec,
                      pl.BlockSpec((rows_per_tile, er.shape[1]), lambda b, i, j: (i, 0)),
                      pl.BlockSpec(ec.shape, lambda b, i, j: (0, 0)),
                      mspec, wspec, wspec],
            out_specs=[xspec, ospec, ospec],
            out_shape=[jax.ShapeDtypeStruct((bsz, s, d), _F32),
                       jax.ShapeDtypeStruct((bsz, s, r), _F32),
                       jax.ShapeDtypeStruct((bsz, s, r), _BF16)],
            scratch_shapes=scratch,
            compiler_params=_params("parallel", "parallel", "arbitrary"),
            name="rg_in_latent",
        )(x, er, ec, mods, w_x, w_gate)
    return pl.pallas_call(
        body, grid=grid,
        in_specs=[xspec, mspec, wspec],
        out_specs=ospec,
        out_shape=jax.ShapeDtypeStruct((bsz, s, r), _F32),
        scratch_shapes=scratch,
        compiler_params=_params("parallel", "parallel", "arbitrary"),
        name="rg_in_context",
    )(x, mods, w_x)


def _rg_scan_body(*refs, reverse, final, ts, nchunks):
    if final:
        (u_ref, up_ref, un_ref, cw_ref, cb_ref, wg_ref, bg_ref, lam_ref, h0_ref, hf_ref, gate_ref,
         out_ref, hl_ref, uext, a_scr, b_scr, h_scr, hs_scr) = refs
    else:
        (u_ref, up_ref, un_ref, cw_ref, cb_ref, wg_ref, bg_ref, lam_ref, h0_ref,
         out_ref, hl_ref, uext, a_scr, b_scr, h_scr) = refs
        hs_scr = out_ref
    step = pl.program_id(2)
    chunk = (nchunks - 1 - step) if reverse else step
    halo = _V7X_SUBLANES

    @pl.when(step == 0)
    def _():
        h_scr[...] = h0_ref[...]

    uext[0:halo, :] = jnp.where(chunk == 0, 0.0, up_ref[...])
    uext[halo:halo + ts, :] = u_ref[...]
    uext[halo + ts:2 * halo + ts, :] = jnp.where(chunk == nchunks - 1, 0.0, un_ref[...])
    pad_lo = (RG_CONV_W - 1) // 2
    xb = cb_ref[...]
    for k in range(RG_CONV_W):
        xb = xb + cw_ref[k:k + 1, :] * uext[pl.ds(halo - pad_lo + k, ts), :]

    c = xb.shape[1]
    gates = _dot(xb.astype(_BF16), wg_ref[...]) + bg_ref[...]
    r_gate = jax.nn.sigmoid(gates[:, :c])
    i_gate = jax.nn.sigmoid(gates[:, c:])
    lam = lam_ref[...]
    softplus_neg_lam = jnp.maximum(-lam, 0.0) + jnp.log1p(jnp.exp(-jnp.abs(lam)))
    log_a = -RG_C * r_gate * softplus_neg_lam
    a = jnp.exp(log_a)
    a_scr[...] = a
    b_scr[...] = jnp.sqrt(jnp.tanh(-log_a) * (1.0 + a * a)) * (i_gate * xb)

    ngroups = ts // _V7X_SUBLANES

    def group(k, h):
        g = (ngroups - 1 - k) if reverse else k
        base = pl.multiple_of(g * _V7X_SUBLANES, _V7X_SUBLANES)
        order = range(_V7X_SUBLANES - 1, -1, -1) if reverse else range(_V7X_SUBLANES)
        for rr in order:
            h = a_scr[pl.ds(base + rr, 1), :] * h + b_scr[pl.ds(base + rr, 1), :]
            hs_scr[pl.ds(base + rr, 1), :] = h
        return h

    h = lax.fori_loop(0, ngroups, group, h_scr[...])
    h_scr[...] = h
    hl_ref[...] = h
    if final:
        out_ref[...] = ((hf_ref[...] + hs_scr[...]) * gate_ref[...].astype(_F32)).astype(_BF16)


def _rg_scan(u, conv_w, conv_b, w_gates, b_gates, lam, h0, reverse, hf=None, gate=None):
    bsz, s, r = u.shape
    c = _RG_PAIR
    ts = min(s, 512)
    nchunks = s // ts
    halo = _V7X_SUBLANES
    final = hf is not None

    def chunk_of(t):
        return (nchunks - 1 - t) if reverse else t

    main = pl.BlockSpec((None, ts, c), lambda b, j, t: (b, chunk_of(t), j))
    prev = pl.BlockSpec((None, halo, c),
                        lambda b, j, t: (b, jnp.maximum(chunk_of(t) * (ts // halo) - 1, 0), j))
    nxt = pl.BlockSpec((None, halo, c),
                       lambda b, j, t: (b, jnp.minimum((chunk_of(t) + 1) * (ts // halo), s // halo - 1), j))
    vec = pl.BlockSpec((1, c), lambda b, j, t: (0, j))
    state = pl.BlockSpec((None, 1, c), lambda b, j, t: (b, 0, j))
    in_specs = [main, prev, nxt,
                pl.BlockSpec((RG_CONV_W, c), lambda b, j, t: (0, j)),
                vec,
                pl.BlockSpec((None, c, 2 * c), lambda b, j, t: (j, 0, 0)),
                pl.BlockSpec((None, 1, 2 * c), lambda b, j, t: (j, 0, 0)),
                vec, state]
    args = [u, u, u, conv_w, conv_b, w_gates, b_gates, lam, h0]
    scratch = [pltpu.VMEM((ts + 2 * halo, c), _F32), pltpu.VMEM((ts, c), _F32),
               pltpu.VMEM((ts, c), _F32), pltpu.VMEM((1, c), _F32)]
    if final:
        in_specs += [main, main]
        args += [hf, gate]
        scratch.append(pltpu.VMEM((ts, c), _F32))
    out_dtype = _BF16 if final else _F32
    return pl.pallas_call(
        functools.partial(_rg_scan_body, reverse=reverse, final=final, ts=ts, nchunks=nchunks),
        grid=(bsz, r // c, nchunks),
        in_specs=in_specs,
        out_specs=[main, state],
        out_shape=[jax.ShapeDtypeStruct((bsz, s, r), out_dtype),
                   jax.ShapeDtypeStruct((bsz, 1, r), _F32)],
        scratch_shapes=scratch,
        compiler_params=_params("parallel", "parallel", "arbitrary"),
        name="rg_scan_%s%s" % ("bwd" if reverse else "fwd", "_final" if final else ""),
    )(*args)


def _mm_res_ln_body(a_ref, w_ref, bias_ref, x_ref, m_ref, g_ref, b_ref, o_ref, *, gate_row, alpha):
    y = _dot(a_ref[...].astype(_BF16), w_ref[...]) + bias_ref[...]
    v = alpha * x_ref[...] + m_ref[gate_row:gate_row + 1, :] * y
    o_ref[...] = _layer_norm(v, g_ref[...], b_ref[...])


def _mm_res_ln(a, w, bias, x, mods, gate_row, ln_g, ln_b, alpha, name):
    bsz, s, d = x.shape
    k = a.shape[-1]
    tm = 256
    row = pl.BlockSpec((1, d), lambda b, i: (0, 0))
    return pl.pallas_call(
        functools.partial(_mm_res_ln_body, gate_row=gate_row, alpha=alpha),
        grid=(bsz, s // tm),
        in_specs=[pl.BlockSpec((None, tm, k), lambda b, i: (b, i, 0)),
                  pl.BlockSpec((k, d), lambda b, i: (0, 0)),
                  row,
                  pl.BlockSpec((None, tm, d), lambda b, i: (b, i, 0)),
                  pl.BlockSpec((None, 6, d), lambda b, i: (b, 0, 0)),
                  row, row],
        out_specs=pl.BlockSpec((None, tm, d), lambda b, i: (b, i, 0)),
        out_shape=jax.ShapeDtypeStruct((bsz, s, d), _F32),
        compiler_params=_params("parallel", "parallel"),
        name=name,
    )(a, w, bias, x, mods, ln_g, ln_b)


def _ffn_body(x_ref, m_ref, w1_ref, w3_ref, w2_ref, g_ref, b_ref, o_ref, h_scr, acc_scr, *, nf, alpha):
    f = pl.program_id(2)

    @pl.when(f == 0)
    def _():
        h_scr[...] = (x_ref[...] * (1.0 + m_ref[4:5, :]) + m_ref[3:4, :]).astype(_BF16)
        acc_scr[...] = jnp.zeros_like(acc_scr)

    h = h_scr[...]
    g = _dot(h, w1_ref[...])
    u = _dot(h, w3_ref[...])
    act = (g * jax.nn.sigmoid(g) * u).astype(_BF16)
    acc_scr[...] += _dot(act, w2_ref[...])

    @pl.when(f == nf - 1)
    def _():
        v = alpha * x_ref[...] + m_ref[5:6, :] * acc_scr[...]
        o_ref[...] = _layer_norm(v, g_ref[...], b_ref[...])


def _ffn(x, mods, w1, w3, w2, ln_g, ln_b, alpha):
    bsz, s, d = x.shape
    dff = w1.shape[1]
    tm, tf = 512, 512
    nf = dff // tf
    row = pl.BlockSpec((1, d), lambda b, i, f: (0, 0))
    xspec = pl.BlockSpec((None, tm, d), lambda b, i, f: (b, i, 0))
    return pl.pallas_call(
        functools.partial(_ffn_body, nf=nf, alpha=alpha),
        grid=(bsz, s // tm, nf),
        in_specs=[xspec,
                  pl.BlockSpec((None, 6, d), lambda b, i, f: (b, 0, 0)),
                  pl.BlockSpec((d, tf), lambda b, i, f: (0, f)),
                  pl.BlockSpec((d, tf), lambda b, i, f: (0, f)),
                  pl.BlockSpec((tf, d), lambda b, i, f: (f, 0)),
                  row, row],
        out_specs=xspec,
        out_shape=jax.ShapeDtypeStruct((bsz, s, d), _F32),
        scratch_shapes=[pltpu.VMEM((tm, d), _BF16), pltpu.VMEM((tm, d), _F32)],
        compiler_params=_params("parallel", "parallel", "arbitrary"),
        name="ffn",
    )(x, mods, w1, w3, w2, ln_g, ln_b)


def _pool_body(x_ref, xp_ref, xn_ref, m_ref, w_ref, pb_ref, ps_ref, g_ref, b_ref, o_ref, hext, y_scr,
               *, tm, nchunks, seq, alpha):
    i = pl.program_id(1)
    halo = _V7X_SUBLANES
    scale = 1.0 + m_ref[1:2, :]
    shift = m_ref[0:1, :]
    hext[0:halo, :] = jnp.where(i == 0, 0.0, xp_ref[...] * scale + shift)
    hext[halo:halo + tm, :] = x_ref[...] * scale + shift
    hext[halo + tm:2 * halo + tm, :] = jnp.where(i == nchunks - 1, 0.0, xn_ref[...] * scale + shift)
    t = i * tm + lax.broadcasted_iota(jnp.int32, (tm, 1), 0)
    gd = x_ref.shape[1] // len(POOL_WINDOWS)
    for g, win in enumerate(POOL_WINDOWS):
        cols = slice(g * gd, (g + 1) * gd)
        lo = jnp.maximum(t - win // 2, 0)
        hi = jnp.minimum(t - win // 2 + win, seq)
        inv = 1.0 / (hi - lo).astype(_F32)
        acc = hext[pl.ds(halo - win // 2, tm), cols]
        for j in range(1, win):
            acc = acc + hext[pl.ds(halo - win // 2 + j, tm), cols]
        p = acc * inv - hext[halo:halo + tm, cols]
        y = _dot(p.astype(_BF16), w_ref[g]) + pb_ref[:, cols]
        y_scr[:, cols] = y * ps_ref[:, cols]
    v = alpha * x_ref[...] + m_ref[2:3, :] * y_scr[...]
    o_ref[...] = _layer_norm(v, g_ref[...], b_ref[...])


def _pool_layer(x, mods, pool_w, pool_b, pool_scale, ln_g, ln_b, alpha):
    bsz, s, d = x.shape
    tm = 512
    nchunks = s // tm
    halo = _V7X_SUBLANES
    row = pl.BlockSpec((1, d), lambda b, i: (0, 0))
    xspec = pl.BlockSpec((None, tm, d), lambda b, i: (b, i, 0))
    return pl.pallas_call(
        functools.partial(_pool_body, tm=tm, nchunks=nchunks, seq=s, alpha=alpha),
        grid=(bsz, nchunks),
        in_specs=[xspec,
                  pl.BlockSpec((None, halo, d), lambda b, i: (b, jnp.maximum(i * (tm // halo) - 1, 0), 0)),
                  pl.BlockSpec((None, halo, d),
                               lambda b, i: (b, jnp.minimum((i + 1) * (tm // halo), s // halo - 1), 0)),
                  pl.BlockSpec((None, 6, d), lambda b, i: (b, 0, 0)),
                  pl.BlockSpec(pool_w.shape, lambda b, i: (0, 0, 0)),
                  row, row, row, row],
        out_specs=xspec,
        out_shape=jax.ShapeDtypeStruct((bsz, s, d), _F32),
        scratch_shapes=[pltpu.VMEM((tm + 2 * halo, d), _F32), pltpu.VMEM((tm, d), _F32)],
        compiler_params=_params("parallel", "parallel"),
        name="pool_mixer",
    )(x, x, x, mods, pool_w, pool_b, pool_scale, ln_g, ln_b)


def _conf_in_body(x_ref, m_ref, wa_ref, wb_ref, ba_ref, bb_ref, o_ref, h_scr):
    @pl.when(pl.program_id(2) == 0)
    def _():
        h_scr[...] = (x_ref[...] * (1.0 + m_ref[1:2, :]) + m_ref[0:1, :]).astype(_BF16)

    h = h_scr[...]
    ua = _dot(h, wa_ref[...]) + ba_ref[...]
    ub = _dot(h, wb_ref[...]) + bb_ref[...]
    o_ref[...] = ua * jax.nn.sigmoid(ub)


def _conf_in(x, mods, w1, b1):
    bsz, s, d = x.shape
    tm, tn = 512, 512
    nj = d // tn
    return pl.pallas_call(
        _conf_in_body,
        grid=(bsz, s // tm, nj),
        in_specs=[pl.BlockSpec((None, tm, d), lambda b, i, j: (b, i, 0)),
                  pl.BlockSpec((None, 6, d), lambda b, i, j: (b, 0, 0)),
                  pl.BlockSpec((d, tn), lambda b, i, j: (0, j)),
                  pl.BlockSpec((d, tn), lambda b, i, j: (0, j + nj)),
                  pl.BlockSpec((1, tn), lambda b, i, j: (0, j)),
                  pl.BlockSpec((1, tn), lambda b, i, j: (0, j + nj))],
        out_specs=pl.BlockSpec((None, tm, tn), lambda b, i, j: (b, i, j)),
        out_shape=jax.ShapeDtypeStruct((bsz, s, d), _F32),
        scratch_shapes=[pltpu.VMEM((tm, d), _BF16)],
        compiler_params=_params("parallel", "parallel", "arbitrary"),
        name="conf_in",
    )(x, mods, w1, w1, b1, b1)


_CONF_HALO = 16
_CONF_ROW_CHUNK = 128


def _conf_out_body(u_ref, up_ref, un_ref, x_ref, m_ref, dw_ref, dwb_ref, cg_ref, cb_ref, w2_ref, b2_ref,
                   g_ref, b_ref, o_ref, uext, conv_scr, *, tm, nchunks, alpha):
    i = pl.program_id(1)
    halo = _CONF_HALO
    d = x_ref.shape[1]
    uext[0:halo, :] = jnp.where(i == 0, 0.0, up_ref[...])
    uext[halo:halo + tm, :] = u_ref[...]
    uext[halo + tm:2 * halo + tm, :] = jnp.where(i == nchunks - 1, 0.0, un_ref[...])
    pad = (CONF_KERNEL - 1) // 2
    rb = _CONF_ROW_CHUNK
    for r0 in range(0, tm, rb):
        for c0 in range(0, d, _V7X_LANES):
            cols = slice(c0, c0 + _V7X_LANES)
            acc = jnp.broadcast_to(dwb_ref[:, cols], (rb, _V7X_LANES))
            for k in range(CONF_KERNEL):
                acc = acc + dw_ref[k:k + 1, cols] * uext[pl.ds(halo - pad + k + r0, rb), cols]
            conv_scr[r0:r0 + rb, cols] = acc
    u2 = _layer_norm(conv_scr[...], cg_ref[...], cb_ref[...])
    u2 = u2 * jax.nn.sigmoid(u2)
    y = _dot(u2.astype(_BF16), w2_ref[...]) + b2_ref[...]
    v = alpha * x_ref[...] + m_ref[2:3, :] * y
    o_ref[...] = _layer_norm(v, g_ref[...], b_ref[...])


def _conf_out(u, x, mods, dw, dwb, cg, cb, w2, b2, ln_g, ln_b, alpha):
    bsz, s, d = x.shape
    tm = 256
    nchunks = s // tm
    halo = _CONF_HALO
    row = pl.BlockSpec((1, d), lambda b, i: (0, 0))
    xspec = pl.BlockSpec((None, tm, d), lambda b, i: (b, i, 0))
    return pl.pallas_call(
        functools.partial(_conf_out_body, tm=tm, nchunks=nchunks, alpha=alpha),
        grid=(bsz, nchunks),
        in_specs=[xspec,
                  pl.BlockSpec((None, halo, d), lambda b, i: (b, jnp.maximum(i * (tm // halo) - 1, 0), 0)),
                  pl.BlockSpec((None, halo, d),
                               lambda b, i: (b, jnp.minimum((i + 1) * (tm // halo), s // halo - 1), 0)),
                  xspec,
                  pl.BlockSpec((None, 6, d), lambda b, i: (b, 0, 0)),
                  pl.BlockSpec(dw.shape, lambda b, i: (0, 0)),
                  row, row, row,
                  pl.BlockSpec((d, d), lambda b, i: (0, 0)),
                  row, row, row],
        out_specs=xspec,
        out_shape=jax.ShapeDtypeStruct((bsz, s, d), _F32),
        scratch_shapes=[pltpu.VMEM((tm + 2 * halo, d), _F32), pltpu.VMEM((tm, d), _F32)],
        compiler_params=_params("parallel", "parallel"),
        name="conf_out",
    )(u, u, u, x, mods, dw, dwb, cg, cb, w2, b2, ln_g, ln_b)


def _ft_in_body(x_ref, m_ref, w_ref, zr_ref, zi_ref, *, groups):
    h = (x_ref[...] * (1.0 + m_ref[1:2, :]) + m_ref[0:1, :]).astype(_BF16)
    gd = x_ref.shape[1] // groups
    for g in range(groups):
        cols = slice(g * gd, (g + 1) * gd)
        z = _dot(h[:, cols], w_ref[...])
        zr_ref[:, cols] = z[:, :gd]
        zi_ref[:, cols] = z[:, gd:]


def _ft_in(x, mods, w_dft):
    bsz, s, d = x.shape
    tm = 512
    xspec = pl.BlockSpec((None, tm, d), lambda b, i: (b, i, 0))
    return pl.pallas_call(
        functools.partial(_ft_in_body, groups=FT_GROUPS),
        grid=(bsz, s // tm),
        in_specs=[xspec,
                  pl.BlockSpec((None, 6, d), lambda b, i: (b, 0, 0)),
                  pl.BlockSpec(w_dft.shape, lambda b, i: (0, 0))],
        out_specs=[xspec, xspec],
        out_shape=[jax.ShapeDtypeStruct((bsz, s, d), _F32)] * 2,
        compiler_params=_params("parallel", "parallel"),
        name="ft_channel_dft",
    )(x, mods, w_dft)


def _ft_seq_body(zr_ref, zi_ref, ma_ref, w2_ref, o_ref, yr_scr, yi_scr, *, n1, n2):
    for s2 in range(n2):
        zr = zr_ref[pl.ds(s2, n1, stride=n2), :]
        zi = zi_ref[pl.ds(s2, n1, stride=n2), :]
        zc = jnp.concatenate([zr, zi], axis=0).astype(_BF16)
        y = _dot(ma_ref[s2], zc)
        yr_scr[pl.ds(s2, n1, stride=n2), :] = y[:n1]
        yi_scr[pl.ds(s2, n1, stride=n2), :] = y[n1:]
    for k1 in range(n1):
        rows = slice(k1 * n2, (k1 + 1) * n2)
        yc = jnp.concatenate([yr_scr[rows, :], yi_scr[rows, :]], axis=0).astype(_BF16)
        o_ref[pl.ds(k1, n2, stride=n1), :] = _dot(w2_ref[...], yc)


def _ft_seq(zr, zi, ma, w2cat, n1, n2):
    bsz, s, d = zr.shape
    tc = _V7X_LANES
    spec = pl.BlockSpec((None, s, tc), lambda b, j: (b, 0, j))
    return pl.pallas_call(
        functools.partial(_ft_seq_body, n1=n1, n2=n2),
        grid=(bsz, d // tc),
        in_specs=[spec, spec,
                  pl.BlockSpec(ma.shape, lambda b, j: (0, 0, 0)),
                  pl.BlockSpec(w2cat.shape, lambda b, j: (0, 0))],
        out_specs=spec,
        out_shape=jax.ShapeDtypeStruct((bsz, s, d), _F32),
        scratch_shapes=[pltpu.VMEM((s, tc), _F32), pltpu.VMEM((s, tc), _F32)],
        compiler_params=_params("parallel", "parallel"),
        name="ft_sequence_dft",
    )(zr, zi, ma, w2cat)


def _ft_tables(s, gd):
    n2 = 128
    n1 = s // n2
    p = np.arange(gd)
    ang = 2.0 * np.pi * ((p[:, None] * p[None, :]) % gd) / gd
    scale = 1.0 / np.sqrt(float(s) * gd)
    w_dft = np.concatenate([np.cos(ang), -np.sin(ang)], axis=1) * scale
    k1 = np.arange(n1)
    s1 = np.arange(n1)
    s2 = np.arange(n2)
    pos = n2 * s1[None, None, :] + s2[:, None, None]
    ang = 2.0 * np.pi * ((k1[None, :, None] * pos) % s) / s
    er, ei = np.cos(ang), np.sin(ang)
    ma = np.concatenate([np.concatenate([er, ei], axis=2),
                         np.concatenate([-ei, er], axis=2)], axis=1)
    k2 = np.arange(n2)
    ang = 2.0 * np.pi * ((k2[:, None] * s2[None, :]) % n2) / n2
    w2cat = np.concatenate([np.cos(ang), np.sin(ang)], axis=1)
    as_bf16 = lambda a: jnp.asarray(a.astype(np.float32)).astype(_BF16)
    return as_bf16(w_dft), as_bf16(ma), as_bf16(w2cat), n1, n2


def _pad_blocks_last(a):
    lead = a.shape[:-1]
    blk = a.shape[-1] // RG_BLOCKS
    a = a.reshape(lead + (RG_BLOCKS, blk))
    a = jnp.pad(a, [(0, 0)] * len(lead) + [(0, 0), (0, _RG_BLOCK_PAD - blk)])
    return a.reshape(lead + (RG_BLOCKS * _RG_BLOCK_PAD,))


def _pair_block_diag(w):
    blk = w.shape[-1]
    w = jnp.pad(w, [(0, 0), (0, _RG_BLOCK_PAD - blk), (0, _RG_BLOCK_PAD - blk)])
    w = w.reshape(RG_BLOCKS // 2, 2, _RG_BLOCK_PAD, _RG_BLOCK_PAD)
    out = jnp.zeros((RG_BLOCKS // 2, 2, _RG_BLOCK_PAD, 2, _RG_BLOCK_PAD), w.dtype)
    out = out.at[:, 0, :, 0, :].set(w[:, 0]).at[:, 1, :, 1, :].set(w[:, 1])
    return out.reshape(RG_BLOCKS // 2, _RG_PAIR, _RG_PAIR)


def _pos_tables(rows, cols, dim):
    quarter = dim // 4
    omega = 1.0 / (POS_THETA ** (jnp.arange(quarter, dtype=_F32) / quarter))
    ar = jnp.arange(rows, dtype=_F32)[:, None] * omega[None]
    ac = jnp.arange(cols, dtype=_F32)[:, None] * omega[None]
    er = jnp.concatenate([jnp.sin(ar), jnp.cos(ar)], axis=-1)
    ec = jnp.concatenate([jnp.sin(ac), jnp.cos(ac)], axis=-1)
    return er, ec


def _rglru_layer(x, ctx, mods, bsz, w_gate, w_x, conv_w, conv_b, wr, br, wi, bi, lam, w_out,
                 ln_g, ln_b, alpha):
    s, d = x.shape[1], x.shape[2]
    w_x_p = _pad_blocks_last(w_x).astype(_BF16)
    w_gate_p = _pad_blocks_last(w_gate).astype(_BF16)
    conv_w_p = _pad_blocks_last(conv_w)
    conv_b_p = _pad_blocks_last(conv_b)[None]
    blk = w_out.shape[0] // RG_BLOCKS
    w_out_p = jnp.pad(w_out.reshape(RG_BLOCKS, blk, d), [(0, 0), (0, _RG_BLOCK_PAD - blk), (0, 0)])
    w_out_p = w_out_p.reshape(RG_BLOCKS * _RG_BLOCK_PAD, d).astype(_BF16)
    npairs = RG_BLOCKS // 2
    er_ec = _pos_tables(s // GRID_W, GRID_W, d)
    x0, xw, gate = _rg_in(x, mods, lambda b: b, w_x_p, w_gate_p, er_ec)
    xwc = _rg_in(ctx, mods, lambda b: bsz, w_x_p, None, None)
    zeros_state = jnp.zeros((bsz, 1, xw.shape[-1]), _F32)
    hf = None
    out = None
    for direction in range(2):
        reverse = direction == 1
        w_gates = jnp.concatenate([_pair_block_diag(wr[direction]), _pair_block_diag(wi[direction])],
                                  axis=-1).astype(_BF16)
        b_gates = jnp.concatenate([_pad_blocks_last(br[direction]).reshape(npairs, 1, _RG_PAIR),
                                   _pad_blocks_last(bi[direction]).reshape(npairs, 1, _RG_PAIR)], axis=-1)
        lam_p = _pad_blocks_last(lam[direction])[None]
        _, h0 = _rg_scan(xwc, conv_w_p, conv_b_p, w_gates, b_gates, lam_p, zeros_state, reverse)
        if not reverse:
            hf, _ = _rg_scan(xw, conv_w_p, conv_b_p, w_gates, b_gates, lam_p, h0, reverse)
        else:
            out, _ = _rg_scan(xw, conv_w_p, conv_b_p, w_gates, b_gates, lam_p, h0, reverse, hf, gate)
    zero_bias = jnp.zeros((1, d), _F32)
    return _mm_res_ln(out, w_out_p, zero_bias, x0, mods, 2, ln_g, ln_b, alpha, "rg_out")


def kernel(x, c, ctx, c_ctx, mod_w, mod_b, ln_g, ln_b, ffn_w1, ffn_w3, ffn_w2, rg_w_gate, rg_w_x, rg_conv_w, rg_conv_b, rg_wr, rg_br, rg_wi, rg_bi, rg_lam, rg_w_out, pool_w, pool_b, pool_scale, cv_w1, cv_b1, cv_dw, cv_dwb, cv_ln_g, cv_ln_b, cv_w2, cv_b2, ft_w, ft_b):
    bsz, s, d = x.shape
    depth = mod_w.shape[0]
    alpha = float((2 * depth) ** 0.25)
    assert depth == N_MIXERS and bsz < _MOD_ROWS, "one layer per mixer kind; the context feeds layer 0 only"

    cond = jnp.zeros((_MOD_ROWS, d), _F32).at[:bsz].set(c).at[bsz].set(c_ctx)
    mods_all = _cond_vectors(cond, mod_w, mod_b).reshape(depth, _MOD_ROWS, 6, d)

    def row(v):
        return v.reshape(1, -1)

    def ffn(xc, i):
        return _ffn(xc, mods_all[i], ffn_w1[i].astype(_BF16), ffn_w3[i].astype(_BF16),
                    ffn_w2[i].astype(_BF16), row(ln_g[i, 1]), row(ln_b[i, 1]), alpha)

    xc = _rglru_layer(x, ctx, mods_all[0], bsz, rg_w_gate[0], rg_w_x[0], rg_conv_w[0], rg_conv_b[0],
                      rg_wr[0], rg_br[0], rg_wi[0], rg_bi[0], rg_lam[0], rg_w_out[0],
                      row(ln_g[0, 0]), row(ln_b[0, 0]), alpha)
    xc = ffn(xc, 0)

    xc = _pool_layer(xc, mods_all[1], pool_w[0].astype(_BF16), row(pool_b[0]), row(pool_scale[0]),
                     row(ln_g[1, 0]), row(ln_b[1, 0]), alpha)
    xc = ffn(xc, 1)

    u = _conf_in(xc, mods_all[2], cv_w1[0].astype(_BF16), row(cv_b1[0]))
    xc = _conf_out(u, xc, mods_all[2], cv_dw[0], row(cv_dwb[0]), row(cv_ln_g[0]), row(cv_ln_b[0]),
                   cv_w2[0].astype(_BF16), row(cv_b2[0]), row(ln_g[2, 0]), row(ln_b[2, 0]), alpha)
    xc = ffn(xc, 2)

    w_dft, ma, w2cat, n1, n2 = _ft_tables(s, d // FT_GROUPS)
    zr, zi = _ft_in(xc, mods_all[3], w_dft)
    f = _ft_seq(zr, zi, ma, w2cat, n1, n2)
    xc = _mm_res_ln(f, ft_w[0].astype(_BF16), row(ft_b[0]), xc, mods_all[3], 2,
                    row(ln_g[3, 0]), row(ln_b[3, 0]), alpha, "ft_out")
    xc = ffn(xc, 3)
    return xc
```

```python
import functools

import numpy as np
import jax
import jax.numpy as jnp
from jax import lax
from jax.experimental import pallas as pl
from jax.experimental.pallas import tpu as pltpu

_F32 = jnp.float32
_BF16 = jnp.bfloat16

GRID_W = 64
N_MIXERS = 4
RG_BLOCKS = 16
RG_CONV_W = 4
RG_C = 8.0
POOL_WINDOWS = (2, 4, 8, 16)
CONF_KERNEL = 31
FT_GROUPS = 4
LN_EPS = 1e-5
POS_THETA = 10000.0

_V7X_LANES = 128
_V7X_SUBLANES = 8
_V7X_VMEM_BYTES = 64 * 1024 * 1024
_VMEM_LIMIT = _V7X_VMEM_BYTES - 12 * 1024 * 1024

_RG_BLOCK_PAD = 192
_RG_PAIR = 2 * _RG_BLOCK_PAD

_MOD_ROWS = 8
_NSEG = _V7X_SUBLANES
_SQRT_FLOOR = 1e-30


def _params(*sem):
    return pltpu.CompilerParams(dimension_semantics=sem, vmem_limit_bytes=_VMEM_LIMIT)


def _layer_norm(v, g, b):
    mu = jnp.mean(v, axis=-1, keepdims=True)
    xc = v - mu
    var = jnp.mean(xc * xc, axis=-1, keepdims=True)
    return xc * lax.rsqrt(var + LN_EPS) * g + b


def _dot(a, b):
    return jnp.dot(a, b, preferred_element_type=_F32)


def _to_segment_major(v):
    rows, c = v.shape
    return pltpu.einshape("mjc->jmc", v.reshape(_NSEG, rows // _NSEG, c)).reshape(rows, c)


def _from_segment_major(v):
    rows, c = v.shape
    return pltpu.einshape("jmc->mjc", v.reshape(rows // _NSEG, _NSEG, c)).reshape(rows, c)


def _neighbour_segment_tile(tile, edge_row, toward_next):
    sub = lax.broadcasted_iota(jnp.int32, tile.shape, 0)
    if toward_next:
        return jnp.where(sub == _NSEG - 1, edge_row, pltpu.roll(tile, shift=_NSEG - 1, axis=0))
    return jnp.where(sub == 0, edge_row, pltpu.roll(tile, shift=1, axis=0))


def _mod_spec(layer, d, row_of_batch=lambda b: b):
    return pl.BlockSpec((None, None, 6, d), lambda b, *_: (layer, row_of_batch(b), 0, 0))


def _vec_spec(index, d):
    return pl.BlockSpec((None, 1, d), lambda *_: (index, 0, 0))


def _cond_body(c_ref, w_ref, b_ref, o_ref):
    cond = c_ref[...]
    cond = cond * jax.nn.sigmoid(cond)
    o_ref[...] = _dot(cond.astype(_BF16), w_ref[...].astype(_BF16)) + b_ref[...]


def _cond_vectors(cond, mod_w, mod_b):
    depth, d, n = mod_w.shape
    tn = 1024
    return pl.pallas_call(
        _cond_body,
        grid=(depth, n // tn),
        in_specs=[pl.BlockSpec((_MOD_ROWS, d), lambda l, j: (0, 0)),
                  pl.BlockSpec((None, d, tn), lambda l, j: (l, 0, j)),
                  pl.BlockSpec((None, 1, tn), lambda l, j: (l, 0, j))],
        out_specs=pl.BlockSpec((None, _MOD_ROWS, tn), lambda l, j: (l, 0, j)),
        out_shape=jax.ShapeDtypeStruct((depth, _MOD_ROWS, n), _F32),
        compiler_params=_params("parallel", "parallel"),
        name="cond_vectors",
    )(cond, mod_w, mod_b.reshape(depth, 1, n))


def _rg_in_body(*refs, latent, tm):
    if latent:
        x_ref, er_ref, ec_ref, m_ref, wx_ref, wg_ref, x0_ref, xw_ref, gate_ref, h_scr = refs
    else:
        x_ref, m_ref, wx_ref, xw_ref, h_scr = refs

    @pl.when(pl.program_id(2) == 0)
    def _():
        if latent:
            half = er_ref.shape[1]
            for r in range(tm // GRID_W):
                rows = slice(r * GRID_W, (r + 1) * GRID_W)
                x0_ref[rows, :half] = x_ref[rows, :half] + er_ref[r:r + 1, :]
                x0_ref[rows, half:] = x_ref[rows, half:] + ec_ref[...]
            x0 = x0_ref[...]
        else:
            x0 = x_ref[...]
        h_scr[...] = (x0 * (1.0 + m_ref[1:2, :]) + m_ref[0:1, :]).astype(_BF16)

    h = h_scr[...]
    xw_ref[...] = _dot(h, wx_ref[...])
    if latent:
        gate_ref[...] = jax.nn.gelu(_dot(h, wg_ref[...])).astype(_BF16)


def _rg_in(x, mods, layer, mod_row, w_x, w_gate, pos_tables):
    bsz, s, d = x.shape
    r = w_x.shape[1]
    latent = pos_tables is not None
    tm = 512 if latent else s
    tn = 512
    grid = (bsz, s // tm, r // tn)
    xspec = pl.BlockSpec((None, tm, d), lambda b, i, j: (b, i, 0))
    mspec = _mod_spec(layer, d, mod_row)
    wspec = pl.BlockSpec((d, tn), lambda b, i, j: (0, j))
    ospec = pl.BlockSpec((None, tm, tn), lambda b, i, j: (b, i, j))
    body = functools.partial(_rg_in_body, latent=latent, tm=tm)
    scratch = [pltpu.VMEM((tm, d), _BF16)]
    if latent:
        er, ec = pos_tables
        rows_per_tile = tm // GRID_W
        return pl.pallas_call(
            body, grid=grid,
            in_specs=[xspec,
                      pl.BlockSpec((rows_per_tile, er.shape[1]), lambda b, i, j: (i, 0)),
                      pl.BlockSpec(ec.shape, lambda b, i, j: (0, 0)),
                      mspec, wspec, wspec],
            out_specs=[xspec, ospec, ospec],
            out_shape=[jax.ShapeDtypeStruct((bsz, s, d), _F32),
                       jax.ShapeDtypeStruct((bsz, s, r), _F32),
                       jax.ShapeDtypeStruct((bsz, s, r), _BF16)],
            scratch_shapes=scratch,
            compiler_params=_params("parallel", "parallel", "arbitrary"),
            name="rg_in_latent",
        )(x, er, ec, mods, w_x, w_gate)
    return pl.pallas_call(
        body, grid=grid,
        in_specs=[xspec, mspec, wspec],
        out_specs=ospec,
        out_shape=jax.ShapeDtypeStruct((bsz, s, r), _F32),
        scratch_shapes=scratch,
        compiler_params=_params("parallel", "parallel", "arbitrary"),
        name="rg_in_context",
    )(x, mods, w_x)


def _rg_scan_body(*refs, reverse, final, ts, nchunks):
    if final:
        (u_ref, up_ref, un_ref, cw_ref, cb_ref, wg_ref, bg_ref, lam_ref, h0_ref, hf_ref, gate_ref,
         out_ref, hl_ref, uperm, xb_scr, a_scr, b_scr, h_scr, hin_scr, nat_scr) = refs
    else:
        (u_ref, up_ref, un_ref, cw_ref, cb_ref, wg_ref, bg_ref, lam_ref, h0_ref,
         out_ref, hl_ref, uperm, xb_scr, a_scr, b_scr, h_scr, hin_scr) = refs
        nat_scr = out_ref
    step = pl.program_id(2)
    chunk = (nchunks - 1 - step) if reverse else step
    halo = _V7X_SUBLANES
    seg_len = ts // _NSEG
    c = u_ref.shape[1]

    @pl.when(step == 0)
    def _():
        h_scr[...] = h0_ref[...]

    lanes = _V7X_LANES
    ncol = c // lanes
    blocks = [slice(i * lanes, (i + 1) * lanes) for i in range(ncol)]

    pad_lo = (RG_CONV_W - 1) // 2
    pad_hi = RG_CONV_W - 1 - pad_lo
    for cols in blocks:
        perm = _to_segment_major(u_ref[:, cols])
        uperm[pad_lo * _NSEG:pad_lo * _NSEG + ts, cols] = perm
        for i in range(-pad_lo, 0):
            edge = jnp.where(chunk == 0, 0.0, up_ref[halo + i:halo + i + 1, cols])
            src_rows = slice((seg_len + i) * _NSEG, (seg_len + i + 1) * _NSEG)
            uperm[(i + pad_lo) * _NSEG:(i + pad_lo + 1) * _NSEG, cols] = _neighbour_segment_tile(
                perm[src_rows, :], edge, toward_next=False)
        for i in range(seg_len, seg_len + pad_hi):
            edge = jnp.where(chunk == nchunks - 1, 0.0, un_ref[i - seg_len:i - seg_len + 1, cols])
            src_rows = slice((i - seg_len) * _NSEG, (i - seg_len + 1) * _NSEG)
            uperm[(i + pad_lo) * _NSEG:(i + pad_lo + 1) * _NSEG, cols] = _neighbour_segment_tile(
                perm[src_rows, :], edge, toward_next=True)
        cw = [jnp.broadcast_to(cw_ref[k:k + 1, cols], (_NSEG, lanes)) for k in range(RG_CONV_W)]
        cb = jnp.broadcast_to(cb_ref[:, cols], (_NSEG, lanes))
        taps = [uperm[q * _NSEG:(q + 1) * _NSEG, cols] for q in range(seg_len + RG_CONV_W - 1)]
        for j in range(seg_len):
            acc = cb
            for k in range(RG_CONV_W):
                acc = acc + cw[k] * taps[j + k]
            xb_scr[j * _NSEG:(j + 1) * _NSEG, cols] = acc

    xb = xb_scr[...]
    g = _dot(xb.astype(_BF16), wg_ref[...]) + bg_ref[...]
    t_r = jnp.tanh(g[:, :c])
    t_i = jnp.tanh(g[:, c:])
    lam = lam_ref[...]
    softplus_neg_lam = jnp.maximum(-lam, 0.0) + jnp.log1p(jnp.exp(-jnp.abs(lam)))
    c0 = (-0.5 * RG_C) * softplus_neg_lam
    log_a = c0 * t_r + c0
    a = jnp.exp(log_a)
    z = jnp.tanh(-log_a) * (1.0 + a * a)
    root = z * lax.rsqrt(jnp.maximum(z, _SQRT_FLOOR))
    a_scr[...] = a
    b_scr[...] = root * ((0.5 * t_i + 0.5) * xb)

    h = [jnp.zeros((_NSEG, lanes), _F32)] * ncol
    p = [jnp.ones((_NSEG, lanes), _F32)] * ncol
    order = range(seg_len - 1, -1, -1) if reverse else range(seg_len)
    for j in order:
        rows = slice(j * _NSEG, (j + 1) * _NSEG)
        for i, cols in enumerate(blocks):
            aj = a_scr[rows, cols]
            h[i] = aj * h[i] + b_scr[rows, cols]
            p[i] = aj * p[i]
            b_scr[rows, cols] = h[i]
            a_scr[rows, cols] = p[i]
    j_last = 0 if reverse else seg_len - 1

    carry = h_scr[...]
    for m in (range(_NSEG - 1, -1, -1) if reverse else range(_NSEG)):
        hin_scr[m:m + 1, :] = carry
        row = j_last * _NSEG + m
        carry = b_scr[row:row + 1, :] + a_scr[row:row + 1, :] * carry
    h_scr[...] = carry
    hl_ref[...] = carry

    for cols in blocks:
        local = b_scr[:, cols].reshape(seg_len, _NSEG, lanes)
        prod = a_scr[:, cols].reshape(seg_len, _NSEG, lanes)
        states = (local + prod * hin_scr[:, cols][None]).reshape(ts, lanes)
        nat_scr[:, cols] = _from_segment_major(states)
    if final:
        out_ref[...] = ((hf_ref[...] + nat_scr[...]) * gate_ref[...].astype(_F32)).astype(_BF16)


def _rg_scan(u, conv_w, conv_b, w_gates, b_gates, lam, h0, reverse, hf=None, gate=None):
    bsz, s, r = u.shape
    c = _RG_PAIR
    ts = min(s, 512)
    nchunks = s // ts
    halo = _V7X_SUBLANES
    final = hf is not None

    def chunk_of(t):
        return (nchunks - 1 - t) if reverse else t

    main = pl.BlockSpec((None, ts, c), lambda b, j, t: (b, chunk_of(t), j))
    prev = pl.BlockSpec((None, halo, c),
                        lambda b, j, t: (b, jnp.maximum(chunk_of(t) * (ts // halo) - 1, 0), j))
    nxt = pl.BlockSpec((None, halo, c),
                       lambda b, j, t: (b, jnp.minimum((chunk_of(t) + 1) * (ts // halo), s // halo - 1), j))
    vec = pl.BlockSpec((1, c), lambda b, j, t: (0, j))
    state = pl.BlockSpec((None, 1, c), lambda b, j, t: (b, 0, j))
    in_specs = [main, prev, nxt,
                pl.BlockSpec((RG_CONV_W, c), lambda b, j, t: (0, j)),
                vec,
                pl.BlockSpec((None, c, 2 * c), lambda b, j, t: (j, 0, 0)),
                pl.BlockSpec((None, 1, 2 * c), lambda b, j, t: (j, 0, 0)),
                vec, state]
    args = [u, u, u, conv_w, conv_b, w_gates, b_gates, lam, h0]
    scratch = [pltpu.VMEM((ts + (RG_CONV_W - 1) * _NSEG, c), _F32), pltpu.VMEM((ts, c), _F32),
               pltpu.VMEM((ts, c), _F32), pltpu.VMEM((ts, c), _F32),
               pltpu.VMEM((1, c), _F32), pltpu.VMEM((_NSEG, c), _F32)]
    if final:
        in_specs += [main, main]
        args += [hf, gate]
        scratch.append(pltpu.VMEM((ts, c), _F32))
    out_dtype = _BF16 if final else _F32
    return pl.pallas_call(
        functools.partial(_rg_scan_body, reverse=reverse, final=final, ts=ts, nchunks=nchunks),
        grid=(bsz, r // c, nchunks),
        in_specs=in_specs,
        out_specs=[main, state],
        out_shape=[jax.ShapeDtypeStruct((bsz, s, r), out_dtype),
                   jax.ShapeDtypeStruct((bsz, 1, r), _F32)],
        scratch_shapes=scratch,
        compiler_params=_params("parallel", "parallel", "arbitrary"),
        name="rg_scan_%s%s" % ("bwd" if reverse else "fwd", "_final" if final else ""),
    )(*args)


def _mm_res_ln_body(a_ref, w_ref, bias_ref, x_ref, m_ref, g_ref, b_ref, o_ref, *, gate_row, alpha):
    y = _dot(a_ref[...].astype(_BF16), w_ref[...]) + bias_ref[...]
    v = alpha * x_ref[...] + m_ref[gate_row:gate_row + 1, :] * y
    o_ref[...] = _layer_norm(v, g_ref[...], b_ref[...])


def _mm_res_ln(a, w, bias, x, mods, layer, gate_row, ln_g, ln_b, alpha, name):
    bsz, s, d = x.shape
    k = a.shape[-1]
    tm = 256
    return pl.pallas_call(
        functools.partial(_mm_res_ln_body, gate_row=gate_row, alpha=alpha),
        grid=(bsz, s // tm),
        in_specs=[pl.BlockSpec((None, tm, k), lambda b, i: (b, i, 0)),
                  pl.BlockSpec((k, d), lambda b, i: (0, 0)),
                  pl.BlockSpec((1, d), lambda b, i: (0, 0)),
                  pl.BlockSpec((None, tm, d), lambda b, i: (b, i, 0)),
                  _mod_spec(layer, d),
                  _vec_spec(2 * layer, d), _vec_spec(2 * layer, d)],
        out_specs=pl.BlockSpec((None, tm, d), lambda b, i: (b, i, 0)),
        out_shape=jax.ShapeDtypeStruct((bsz, s, d), _F32),
        compiler_params=_params("parallel", "parallel"),
        name=name,
    )(a, w, bias, x, mods, ln_g, ln_b)


def _ffn_body(x_ref, m_ref, w1_ref, w3_ref, w2_ref, g_ref, b_ref, o_ref, h_scr, acc_scr, *, nf, alpha):
    f = pl.program_id(2)

    @pl.when(f == 0)
    def _():
        h_scr[...] = (x_ref[...] * (1.0 + m_ref[4:5, :]) + m_ref[3:4, :]).astype(_BF16)
        acc_scr[...] = jnp.zeros_like(acc_scr)

    h = h_scr[...]
    g = _dot(h, w1_ref[...])
    u = _dot(h, w3_ref[...])
    act = (g * jax.nn.sigmoid(g) * u).astype(_BF16)
    acc_scr[...] += _dot(act, w2_ref[...])

    @pl.when(f == nf - 1)
    def _():
        v = alpha * x_ref[...] + m_ref[5:6, :] * acc_scr[...]
        o_ref[...] = _layer_norm(v, g_ref[...], b_ref[...])


def _ffn(x, mods, layer, w1, w3, w2, ln_g, ln_b, alpha):
    bsz, s, d = x.shape
    dff = w1.shape[2]
    tm, tf = 512, 512
    nf = dff // tf
    xspec = pl.BlockSpec((None, tm, d), lambda b, i, f: (b, i, 0))
    return pl.pallas_call(
        functools.partial(_ffn_body, nf=nf, alpha=alpha),
        grid=(bsz, s // tm, nf),
        in_specs=[xspec,
                  _mod_spec(layer, d),
                  pl.BlockSpec((None, d, tf), lambda b, i, f: (layer, 0, f)),
                  pl.BlockSpec((None, d, tf), lambda b, i, f: (layer, 0, f)),
                  pl.BlockSpec((None, tf, d), lambda b, i, f: (layer, f, 0)),
                  _vec_spec(2 * layer + 1, d), _vec_spec(2 * layer + 1, d)],
        out_specs=xspec,
        out_shape=jax.ShapeDtypeStruct((bsz, s, d), _F32),
        scratch_shapes=[pltpu.VMEM((tm, d), _BF16), pltpu.VMEM((tm, d), _F32)],
        compiler_params=_params("parallel", "parallel", "arbitrary"),
        name="ffn",
    )(x, mods, w1, w3, w2, ln_g, ln_b)


def _pool_body(x_ref, xp_ref, xn_ref, m_ref, w_ref, pb_ref, ps_ref, g_ref, b_ref, o_ref, hext, y_scr,
               *, tm, nchunks, seq, alpha):
    i = pl.program_id(1)
    halo = _V7X_SUBLANES
    scale = 1.0 + m_ref[1:2, :]
    shift = m_ref[0:1, :]
    hext[0:halo, :] = jnp.where(i == 0, 0.0, xp_ref[...] * scale + shift)
    hext[halo:halo + tm, :] = x_ref[...] * scale + shift
    hext[halo + tm:2 * halo + tm, :] = jnp.where(i == nchunks - 1, 0.0, xn_ref[...] * scale + shift)
    t = i * tm + lax.broadcasted_iota(jnp.int32, (tm, 1), 0)
    gd = x_ref.shape[1] // len(POOL_WINDOWS)
    for g, win in enumerate(POOL_WINDOWS):
        cols = slice(g * gd, (g + 1) * gd)
        lo = jnp.maximum(t - win // 2, 0)
        hi = jnp.minimum(t - win // 2 + win, seq)
        inv = 1.0 / (hi - lo).astype(_F32)
        acc = hext[pl.ds(halo - win // 2, tm), cols]
        for j in range(1, win):
            acc = acc + hext[pl.ds(halo - win // 2 + j, tm), cols]
        p = acc * inv - hext[halo:halo + tm, cols]
        y = _dot(p.astype(_BF16), w_ref[g]) + pb_ref[:, cols]
        y_scr[:, cols] = y * ps_ref[:, cols]
    v = alpha * x_ref[...] + m_ref[2:3, :] * y_scr[...]
    o_ref[...] = _layer_norm(v, g_ref[...], b_ref[...])


def _pool_layer(x, mods, layer, pool_w, pool_b, pool_scale, ln_g, ln_b, alpha):
    bsz, s, d = x.shape
    tm = 512
    nchunks = s // tm
    halo = _V7X_SUBLANES
    row = pl.BlockSpec((1, d), lambda b, i: (0, 0))
    xspec = pl.BlockSpec((None, tm, d), lambda b, i: (b, i, 0))
    return pl.pallas_call(
        functools.partial(_pool_body, tm=tm, nchunks=nchunks, seq=s, alpha=alpha),
        grid=(bsz, nchunks),
        in_specs=[xspec,
                  pl.BlockSpec((None, halo, d), lambda b, i: (b, jnp.maximum(i * (tm // halo) - 1, 0), 0)),
                  pl.BlockSpec((None, halo, d),
                               lambda b, i: (b, jnp.minimum((i + 1) * (tm // halo), s // halo - 1), 0)),
                  _mod_spec(layer, d),
                  pl.BlockSpec(pool_w.shape, lambda b, i: (0, 0, 0)),
                  row, row,
                  _vec_spec(2 * layer, d), _vec_spec(2 * layer, d)],
        out_specs=xspec,
        out_shape=jax.ShapeDtypeStruct((bsz, s, d), _F32),
        scratch_shapes=[pltpu.VMEM((tm + 2 * halo, d), _F32), pltpu.VMEM((tm, d), _F32)],
        compiler_params=_params("parallel", "parallel"),
        name="pool_mixer",
    )(x, x, x, mods, pool_w, pool_b, pool_scale, ln_g, ln_b)


def _conf_in_body(x_ref, m_ref, wa_ref, wb_ref, ba_ref, bb_ref, o_ref, h_scr):
    @pl.when(pl.program_id(2) == 0)
    def _():
        h_scr[...] = (x_ref[...] * (1.0 + m_ref[1:2, :]) + m_ref[0:1, :]).astype(_BF16)

    h = h_scr[...]
    ua = _dot(h, wa_ref[...]) + ba_ref[...]
    ub = _dot(h, wb_ref[...]) + bb_ref[...]
    o_ref[...] = ua * jax.nn.sigmoid(ub)


def _conf_in(x, mods, layer, w1, b1):
    bsz, s, d = x.shape
    tm, tn = 512, 512
    nj = d // tn
    return pl.pallas_call(
        _conf_in_body,
        grid=(bsz, s // tm, nj),
        in_specs=[pl.BlockSpec((None, tm, d), lambda b, i, j: (b, i, 0)),
                  _mod_spec(layer, d),
                  pl.BlockSpec((d, tn), lambda b, i, j: (0, j)),
                  pl.BlockSpec((d, tn), lambda b, i, j: (0, j + nj)),
                  pl.BlockSpec((1, tn), lambda b, i, j: (0, j)),
                  pl.BlockSpec((1, tn), lambda b, i, j: (0, j + nj))],
        out_specs=pl.BlockSpec((None, tm, tn), lambda b, i, j: (b, i, j)),
        out_shape=jax.ShapeDtypeStruct((bsz, s, d), _F32),
        scratch_shapes=[pltpu.VMEM((tm, d), _BF16)],
        compiler_params=_params("parallel", "parallel", "arbitrary"),
        name="conf_in",
    )(x, mods, w1, w1, b1, b1)


_CONF_HALO = 16


def _conf_out_body(u_ref, up_ref, un_ref, x_ref, m_ref, dw_ref, cg_ref, cb_ref, w2_ref, b2_ref,
                   g_ref, b_ref, o_ref, uslab, pslab, nslab, uperm, conv_scr, *, tm, nchunks, alpha):
    i = pl.program_id(1)
    halo = _CONF_HALO
    lanes = _V7X_LANES
    d = x_ref.shape[1]
    ncol = d // lanes
    seg_len = tm // _NSEG
    pad = (CONF_KERNEL - 1) // 2
    first = i == 0
    last = i == nchunks - 1
    for cb_ in range(ncol):
        cols = slice(cb_ * lanes, (cb_ + 1) * lanes)
        uslab[cb_] = u_ref[:, cols]
        pslab[cb_] = jnp.where(first, 0.0, up_ref[:, cols])
        nslab[cb_] = jnp.where(last, 0.0, un_ref[:, cols])

    def lane_block(cb_, carry):
        perm = _to_segment_major(uslab[cb_])
        uperm[pad * _NSEG:pad * _NSEG + tm, :] = perm
        for t in range(-pad, 0):
            src_rows = slice((seg_len + t) * _NSEG, (seg_len + t + 1) * _NSEG)
            uperm[(t + pad) * _NSEG:(t + pad + 1) * _NSEG, :] = _neighbour_segment_tile(
                perm[src_rows, :], pslab[cb_, halo + t:halo + t + 1, :], toward_next=False)
        for t in range(seg_len, seg_len + pad):
            src_rows = slice((t - seg_len) * _NSEG, (t - seg_len + 1) * _NSEG)
            uperm[(t + pad) * _NSEG:(t + pad + 1) * _NSEG, :] = _neighbour_segment_tile(
                perm[src_rows, :], nslab[cb_, t - seg_len:t - seg_len + 1, :], toward_next=True)
        w = [jnp.broadcast_to(dw_ref[cb_, k:k + 1, :], (_NSEG, lanes)) for k in range(CONF_KERNEL)]
        bias = jnp.broadcast_to(dw_ref[cb_, CONF_KERNEL:CONF_KERNEL + 1, :], (_NSEG, lanes))
        for j in range(0, seg_len, 2):
            acc0 = bias
            acc1 = bias
            for q in range(j, j + CONF_KERNEL + 1):
                tap = uperm[q * _NSEG:(q + 1) * _NSEG, :]
                if q - j < CONF_KERNEL:
                    acc0 = acc0 + w[q - j] * tap
                if q - j >= 1:
                    acc1 = acc1 + w[q - j - 1] * tap
            conv_scr[cb_, j * _NSEG:(j + 1) * _NSEG, :] = acc0
            conv_scr[cb_, (j + 1) * _NSEG:(j + 2) * _NSEG, :] = acc1
        return carry

    lax.fori_loop(0, ncol, lane_block, 0)

    cv = jnp.concatenate([conv_scr[cb_] for cb_ in range(ncol)], axis=-1)
    u2 = _layer_norm(cv, cg_ref[...], cb_ref[...])
    u2 = u2 * jax.nn.sigmoid(u2)
    y = _from_segment_major(_dot(u2.astype(_BF16), w2_ref[...]) + b2_ref[...])
    v = alpha * x_ref[...] + m_ref[2:3, :] * y
    o_ref[...] = _layer_norm(v, g_ref[...], b_ref[...])


def _conf_out(u, x, mods, layer, dw_blocks, cg, cb, w2, b2, ln_g, ln_b, alpha):
    bsz, s, d = x.shape
    tm = 256
    nchunks = s // tm
    halo = _CONF_HALO
    ncol = d // _V7X_LANES
    row = pl.BlockSpec((1, d), lambda b, i: (0, 0))
    xspec = pl.BlockSpec((None, tm, d), lambda b, i: (b, i, 0))
    return pl.pallas_call(
        functools.partial(_conf_out_body, tm=tm, nchunks=nchunks, alpha=alpha),
        grid=(bsz, nchunks),
        in_specs=[xspec,
                  pl.BlockSpec((None, halo, d), lambda b, i: (b, jnp.maximum(i * (tm // halo) - 1, 0), 0)),
                  pl.BlockSpec((None, halo, d),
                               lambda b, i: (b, jnp.minimum((i + 1) * (tm // halo), s // halo - 1), 0)),
                  xspec,
                  _mod_spec(layer, d),
                  pl.BlockSpec(dw_blocks.shape, lambda b, i: (0, 0, 0)),
                  row, row,
                  pl.BlockSpec((d, d), lambda b, i: (0, 0)),
                  row,
                  _vec_spec(2 * layer, d), _vec_spec(2 * layer, d)],
        out_specs=xspec,
        out_shape=jax.ShapeDtypeStruct((bsz, s, d), _F32),
        scratch_shapes=[pltpu.VMEM((ncol, tm, _V7X_LANES), _F32),
                        pltpu.VMEM((ncol, halo, _V7X_LANES), _F32),
                        pltpu.VMEM((ncol, halo, _V7X_LANES), _F32),
                        pltpu.VMEM((tm + (CONF_KERNEL - 1) * _NSEG, _V7X_LANES), _F32),
                        pltpu.VMEM((ncol, tm, _V7X_LANES), _F32)],
        compiler_params=_params("parallel", "parallel"),
        name="conf_out",
    )(u, u, u, x, mods, dw_blocks, cg, cb, w2, b2, ln_g, ln_b)


def _ft_in_body(x_ref, m_ref, w_ref, zr_ref, zi_ref, *, groups):
    h = (x_ref[...] * (1.0 + m_ref[1:2, :]) + m_ref[0:1, :]).astype(_BF16)
    gd = x_ref.shape[1] // groups
    for g in range(groups):
        cols = slice(g * gd, (g + 1) * gd)
        z = _dot(h[:, cols], w_ref[...])
        zr_ref[:, cols] = z[:, :gd]
        zi_ref[:, cols] = z[:, gd:]


def _ft_in(x, mods, layer, w_dft):
    bsz, s, d = x.shape
    tm = 512
    xspec = pl.BlockSpec((None, tm, d), lambda b, i: (b, i, 0))
    return pl.pallas_call(
        functools.partial(_ft_in_body, groups=FT_GROUPS),
        grid=(bsz, s // tm),
        in_specs=[xspec, _mod_spec(layer, d),
                  pl.BlockSpec(w_dft.shape, lambda b, i: (0, 0))],
        out_specs=[xspec, xspec],
        out_shape=[jax.ShapeDtypeStruct((bsz, s, d), _F32)] * 2,
        compiler_params=_params("parallel", "parallel"),
        name="ft_channel_dft",
    )(x, mods, w_dft)


def _swap_major(v, n_major):
    rows, c = v.shape
    n_minor = rows // n_major
    return pltpu.einshape("abc->bac", v.reshape(n_major, n_minor, c)).reshape(rows, c)


def _ft_seq_body(zr_ref, zi_ref, ma_ref, w2_ref, o_ref, pr_scr, pi_scr, yr_scr, yi_scr, *, n1, n2):
    pr_scr[...] = _swap_major(zr_ref[...], n1)
    pi_scr[...] = _swap_major(zi_ref[...], n1)
    for s2 in range(n2):
        rows = slice(s2 * n1, (s2 + 1) * n1)
        zc = jnp.concatenate([pr_scr[rows, :], pi_scr[rows, :]], axis=0).astype(_BF16)
        y = _dot(ma_ref[s2], zc)
        yr_scr[rows, :] = y[:n1]
        yi_scr[rows, :] = y[n1:]
    pr_scr[...] = _swap_major(yr_scr[...], n2)
    pi_scr[...] = _swap_major(yi_scr[...], n2)
    for k1 in range(n1):
        rows = slice(k1 * n2, (k1 + 1) * n2)
        yc = jnp.concatenate([pr_scr[rows, :], pi_scr[rows, :]], axis=0).astype(_BF16)
        yr_scr[rows, :] = _dot(w2_ref[...], yc)
    o_ref[...] = _swap_major(yr_scr[...], n1)


def _ft_seq(zr, zi, ma, w2cat, n1, n2):
    bsz, s, d = zr.shape
    tc = _V7X_LANES
    spec = pl.BlockSpec((None, s, tc), lambda b, j: (b, 0, j))
    return pl.pallas_call(
        functools.partial(_ft_seq_body, n1=n1, n2=n2),
        grid=(bsz, d // tc),
        in_specs=[spec, spec,
                  pl.BlockSpec(ma.shape, lambda b, j: (0, 0, 0)),
                  pl.BlockSpec(w2cat.shape, lambda b, j: (0, 0))],
        out_specs=spec,
        out_shape=jax.ShapeDtypeStruct((bsz, s, d), _F32),
        scratch_shapes=[pltpu.VMEM((s, tc), _F32)] * 4,
        compiler_params=_params("parallel", "parallel"),
        name="ft_sequence_dft",
    )(zr, zi, ma, w2cat)


def _ft_tables(s, gd):
    n2 = 128
    n1 = s // n2
    p = np.arange(gd)
    ang = 2.0 * np.pi * ((p[:, None] * p[None, :]) % gd) / gd
    scale = 1.0 / np.sqrt(float(s) * gd)
    w_dft = np.concatenate([np.cos(ang), -np.sin(ang)], axis=1) * scale
    k1 = np.arange(n1)
    s1 = np.arange(n1)
    s2 = np.arange(n2)
    pos = n2 * s1[None, None, :] + s2[:, None, None]
    ang = 2.0 * np.pi * ((k1[None, :, None] * pos) % s) / s
    er, ei = np.cos(ang), np.sin(ang)
    ma = np.concatenate([np.concatenate([er, ei], axis=2),
                         np.concatenate([-ei, er], axis=2)], axis=1)
    k2 = np.arange(n2)
    ang = 2.0 * np.pi * ((k2[:, None] * s2[None, :]) % n2) / n2
    w2cat = np.concatenate([np.cos(ang), np.sin(ang)], axis=1)
    as_bf16 = lambda a: jnp.asarray(a.astype(np.float32)).astype(_BF16)
    return as_bf16(w_dft), as_bf16(ma), as_bf16(w2cat), n1, n2


def _pad_blocks_last(a):
    lead = a.shape[:-1]
    blk = a.shape[-1] // RG_BLOCKS
    a = a.reshape(lead + (RG_BLOCKS, blk))
    a = jnp.pad(a, [(0, 0)] * len(lead) + [(0, 0), (0, _RG_BLOCK_PAD - blk)])
    return a.reshape(lead + (RG_BLOCKS * _RG_BLOCK_PAD,))


def _pair_block_diag(w):
    blk = w.shape[-1]
    w = jnp.pad(w, [(0, 0), (0, _RG_BLOCK_PAD - blk), (0, _RG_BLOCK_PAD - blk)])
    w = w.reshape(RG_BLOCKS // 2, 2, _RG_BLOCK_PAD, _RG_BLOCK_PAD)
    zero = jnp.zeros_like(w[:, 0])
    top = jnp.concatenate([w[:, 0], zero], axis=-1)
    bottom = jnp.concatenate([zero, w[:, 1]], axis=-1)
    return jnp.concatenate([top, bottom], axis=1)


def _pos_tables(rows, cols, dim):
    quarter = dim // 4
    omega = 1.0 / (POS_THETA ** (jnp.arange(quarter, dtype=_F32) / quarter))
    ar = jnp.arange(rows, dtype=_F32)[:, None] * omega[None]
    ac = jnp.arange(cols, dtype=_F32)[:, None] * omega[None]
    er = jnp.concatenate([jnp.sin(ar), jnp.cos(ar)], axis=-1)
    ec = jnp.concatenate([jnp.sin(ac), jnp.cos(ac)], axis=-1)
    return er, ec


def _rglru_layer(x, ctx, mods, layer, bsz, w_gate, w_x, conv_w, conv_b, wr, br, wi, bi, lam, w_out,
                 ln_g, ln_b, alpha):
    s, d = x.shape[1], x.shape[2]
    w_x_p = _pad_blocks_last(w_x).astype(_BF16)
    w_gate_p = _pad_blocks_last(w_gate).astype(_BF16)
    conv_w_p = _pad_blocks_last(conv_w)
    conv_b_p = _pad_blocks_last(conv_b)[None]
    blk = w_out.shape[0] // RG_BLOCKS
    w_out_p = jnp.pad(w_out.reshape(RG_BLOCKS, blk, d), [(0, 0), (0, _RG_BLOCK_PAD - blk), (0, 0)])
    w_out_p = w_out_p.reshape(RG_BLOCKS * _RG_BLOCK_PAD, d).astype(_BF16)
    npairs = RG_BLOCKS // 2
    er_ec = _pos_tables(s // GRID_W, GRID_W, d)
    x0, xw, gate = _rg_in(x, mods, layer, lambda b: b, w_x_p, w_gate_p, er_ec)
    xwc = _rg_in(ctx, mods, layer, lambda b: bsz, w_x_p, None, None)
    zeros_state = jnp.zeros((bsz, 1, xw.shape[-1]), _F32)
    hf = None
    out = None
    for direction in range(2):
        reverse = direction == 1
        w_gates = (0.5 * jnp.concatenate([_pair_block_diag(wr[direction]), _pair_block_diag(wi[direction])],
                                         axis=-1)).astype(_BF16)
        b_gates = 0.5 * jnp.concatenate([_pad_blocks_last(br[direction]).reshape(npairs, 1, _RG_PAIR),
                                         _pad_blocks_last(bi[direction]).reshape(npairs, 1, _RG_PAIR)], axis=-1)
        lam_p = _pad_blocks_last(lam[direction])[None]
        _, h0 = _rg_scan(xwc, conv_w_p, conv_b_p, w_gates, b_gates, lam_p, zeros_state, reverse)
        if not reverse:
            hf, _ = _rg_scan(xw, conv_w_p, conv_b_p, w_gates, b_gates, lam_p, h0, reverse)
        else:
            out, _ = _rg_scan(xw, conv_w_p, conv_b_p, w_gates, b_gates, lam_p, h0, reverse, hf, gate)
    zero_bias = jnp.zeros((1, d), _F32)
    return _mm_res_ln(out, w_out_p, zero_bias, x0, mods, layer, 2, ln_g, ln_b, alpha, "rg_out")


def kernel(x, c, ctx, c_ctx, mod_w, mod_b, ln_g, ln_b, ffn_w1, ffn_w3, ffn_w2, rg_w_gate, rg_w_x, rg_conv_w, rg_conv_b, rg_wr, rg_br, rg_wi, rg_bi, rg_lam, rg_w_out, pool_w, pool_b, pool_scale, cv_w1, cv_b1, cv_dw, cv_dwb, cv_ln_g, cv_ln_b, cv_w2, cv_b2, ft_w, ft_b):
    bsz, s, d = x.shape
    depth = mod_w.shape[0]
    alpha = float((2 * depth) ** 0.25)
    assert depth == N_MIXERS and bsz < _MOD_ROWS, "one layer per mixer kind; the context feeds layer 0 only"

    cond = jnp.zeros((_MOD_ROWS, d), _F32).at[:bsz].set(c).at[bsz].set(c_ctx)
    mods = _cond_vectors(cond, mod_w, mod_b).reshape(depth, _MOD_ROWS, 6, d)
    ln_g3 = ln_g.reshape(depth * 2, 1, d)
    ln_b3 = ln_b.reshape(depth * 2, 1, d)
    w1_all = ffn_w1.astype(_BF16)
    w3_all = ffn_w3.astype(_BF16)
    w2_all = ffn_w2.astype(_BF16)

    def row(v):
        return v.reshape(1, -1)

    def ffn(xc, layer):
        return _ffn(xc, mods, layer, w1_all, w3_all, w2_all, ln_g3, ln_b3, alpha)

    xc = _rglru_layer(x, ctx, mods, 0, bsz, rg_w_gate[0], rg_w_x[0], rg_conv_w[0], rg_conv_b[0],
                      rg_wr[0], rg_br[0], rg_wi[0], rg_bi[0], rg_lam[0], rg_w_out[0], ln_g3, ln_b3, alpha)
    xc = ffn(xc, 0)

    xc = _pool_layer(xc, mods, 1, pool_w[0].astype(_BF16), row(pool_b[0]), row(pool_scale[0]),
                     ln_g3, ln_b3, alpha)
    xc = ffn(xc, 1)

    u = _conf_in(xc, mods, 2, cv_w1[0].astype(_BF16), row(cv_b1[0]))
    ncol = d // _V7X_LANES
    dw_blocks = jnp.concatenate([cv_dw[0], cv_dwb[0][None]], axis=0)
    dw_blocks = dw_blocks.reshape(CONF_KERNEL + 1, ncol, _V7X_LANES).transpose(1, 0, 2)
    xc = _conf_out(u, xc, mods, 2, dw_blocks, row(cv_ln_g[0]), row(cv_ln_b[0]),
                   cv_w2[0].astype(_BF16), row(cv_b2[0]), ln_g3, ln_b3, alpha)
    xc = ffn(xc, 2)

    w_dft, ma, w2cat, n1, n2 = _ft_tables(s, d // FT_GROUPS)
    zr, zi = _ft_in(xc, mods, 3, w_dft)
    f = _ft_seq(zr, zi, ma, w2cat, n1, n2)
    xc = _mm_res_ln(f, ft_w[0].astype(_BF16), row(ft_b[0]), xc, mods, 3, 2, ln_g3, ln_b3, alpha, "ft_out")
    xc = ffn(xc, 3)
    return xc
```

```python
import functools

import numpy as np
import jax
import jax.numpy as jnp
from jax import lax
from jax.experimental import pallas as pl
from jax.experimental.pallas import tpu as pltpu

_F32 = jnp.float32
_BF16 = jnp.bfloat16

GRID_W = 64
N_MIXERS = 4
RG_BLOCKS = 16
RG_CONV_W = 4
RG_C = 8.0
POOL_WINDOWS = (2, 4, 8, 16)
CONF_KERNEL = 31
FT_GROUPS = 4
LN_EPS = 1e-5
POS_THETA = 10000.0

_V7X_LANES = 128
_V7X_SUBLANES = 8
_V7X_VMEM_BYTES = 64 * 1024 * 1024
_VMEM_LIMIT = _V7X_VMEM_BYTES - 12 * 1024 * 1024
_VMEM_LIMIT_FFN = _V7X_VMEM_BYTES - 4 * 1024 * 1024

_RG_BLOCK_PAD = 192
_RG_PAIR = 2 * _RG_BLOCK_PAD

_MOD_ROWS = 8
_NSEG = _V7X_SUBLANES
_SQRT_FLOOR = 1e-30


def _params(*sem, vmem=_VMEM_LIMIT):
    return pltpu.CompilerParams(dimension_semantics=sem, vmem_limit_bytes=vmem)


def _layer_norm(v, g, b):
    mu = jnp.mean(v, axis=-1, keepdims=True)
    xc = v - mu
    var = jnp.mean(xc * xc, axis=-1, keepdims=True)
    return xc * lax.rsqrt(var + LN_EPS) * g + b


def _dot(a, b):
    return jnp.dot(a, b, preferred_element_type=_F32)


def _to_segment_major(v):
    rows, c = v.shape
    return pltpu.einshape("mjc->jmc", v.reshape(_NSEG, rows // _NSEG, c)).reshape(rows, c)


def _from_segment_major(v):
    rows, c = v.shape
    return pltpu.einshape("jmc->mjc", v.reshape(rows // _NSEG, _NSEG, c)).reshape(rows, c)


def _neighbour_segment_tile(tile, edge_row, toward_next):
    sub = lax.broadcasted_iota(jnp.int32, tile.shape, 0)
    if toward_next:
        return jnp.where(sub == _NSEG - 1, edge_row, pltpu.roll(tile, shift=_NSEG - 1, axis=0))
    return jnp.where(sub == 0, edge_row, pltpu.roll(tile, shift=1, axis=0))


def _mod_spec(layer, d, row_of_batch=lambda b: b):
    return pl.BlockSpec((None, None, 6, d), lambda b, *_: (layer, row_of_batch(b), 0, 0))


def _vec_spec(index, d):
    return pl.BlockSpec((None, 1, d), lambda *_: (index, 0, 0))


def _cond_body(c_ref, w_ref, b_ref, o_ref):
    cond = c_ref[...]
    cond = cond * jax.nn.sigmoid(cond)
    o_ref[...] = _dot(cond.astype(_BF16), w_ref[...].astype(_BF16)) + b_ref[...]


def _cond_vectors(cond, mod_w, mod_b):
    depth, d, n = mod_w.shape
    tn = 1024
    return pl.pallas_call(
        _cond_body,
        grid=(depth, n // tn),
        in_specs=[pl.BlockSpec((_MOD_ROWS, d), lambda l, j: (0, 0)),
                  pl.BlockSpec((None, d, tn), lambda l, j: (l, 0, j)),
                  pl.BlockSpec((None, 1, tn), lambda l, j: (l, 0, j))],
        out_specs=pl.BlockSpec((None, _MOD_ROWS, tn), lambda l, j: (l, 0, j)),
        out_shape=jax.ShapeDtypeStruct((depth, _MOD_ROWS, n), _F32),
        compiler_params=_params("parallel", "parallel"),
        name="cond_vectors",
    )(cond, mod_w, mod_b.reshape(depth, 1, n))


def _rg_in_body(*refs, latent, tm):
    if latent:
        x_ref, er_ref, ec_ref, m_ref, wx_ref, wg_ref, x0_ref, xw_ref, gate_ref, h_scr = refs
    else:
        x_ref, m_ref, wx_ref, xw_ref, h_scr = refs

    @pl.when(pl.program_id(2) == 0)
    def _():
        if latent:
            half = er_ref.shape[1]
            for r in range(tm // GRID_W):
                rows = slice(r * GRID_W, (r + 1) * GRID_W)
                x0_ref[rows, :half] = x_ref[rows, :half] + er_ref[r:r + 1, :]
                x0_ref[rows, half:] = x_ref[rows, half:] + ec_ref[...]
            x0 = x0_ref[...]
        else:
            x0 = x_ref[...]
        h_scr[...] = (x0 * (1.0 + m_ref[1:2, :]) + m_ref[0:1, :]).astype(_BF16)

    h = h_scr[...]
    xw_ref[...] = _dot(h, wx_ref[...])
    if latent:
        gate_ref[...] = jax.nn.gelu(_dot(h, wg_ref[...])).astype(_BF16)


def _rg_in(x, mods, layer, mod_row, w_x, w_gate, pos_tables):
    bsz, s, d = x.shape
    r = w_x.shape[1]
    latent = pos_tables is not None
    tm = 512 if latent else s
    tn = 1024
    grid = (bsz, s // tm, r // tn)
    xspec = pl.BlockSpec((None, tm, d), lambda b, i, j: (b, i, 0))
    mspec = _mod_spec(layer, d, mod_row)
    wspec = pl.BlockSpec((d, tn), lambda b, i, j: (0, j))
    ospec = pl.BlockSpec((None, tm, tn), lambda b, i, j: (b, i, j))
    body = functools.partial(_rg_in_body, latent=latent, tm=tm)
    scratch = [pltpu.VMEM((tm, d), _BF16)]
    if latent:
        er, ec = pos_tables
        rows_per_tile = tm // GRID_W
        return pl.pallas_call(
            body, grid=grid,
            in_specs=[xspec,
                      pl.BlockSpec((rows_per_tile, er.shape[1]), lambda b, i, j: (i, 0)),
                      pl.BlockSpec(ec.shape, lambda b, i, j: (0, 0)),
                      mspec, wspec, wspec],
            out_specs=[xspec, ospec, ospec],
            out_shape=[jax.ShapeDtypeStruct((bsz, s, d), _F32),
                       jax.ShapeDtypeStruct((bsz, s, r), _F32),
                       jax.ShapeDtypeStruct((bsz, s, r), _BF16)],
            scratch_shapes=scratch,
            compiler_params=_params("parallel", "parallel", "arbitrary"),
            name="rg_in_latent",
        )(x, er, ec, mods, w_x, w_gate)
    return pl.pallas_call(
        body, grid=grid,
        in_specs=[xspec, mspec, wspec],
        out_specs=ospec,
        out_shape=jax.ShapeDtypeStruct((bsz, s, r), _F32),
        scratch_shapes=scratch,
        compiler_params=_params("parallel", "parallel", "arbitrary"),
        name="rg_in_context",
    )(x, mods, w_x)


def _rg_scan_body(*refs, reverse, final, ts, nchunks):
    if final:
        (u_ref, up_ref, un_ref, cw_ref, cb_ref, wg_ref, bg_ref, lam_ref, h0_ref, hf_ref, gate_ref,
         out_ref, hl_ref, uperm, xb_scr, a_scr, b_scr, h_scr, hin_scr, nat_scr) = refs
    else:
        (u_ref, up_ref, un_ref, cw_ref, cb_ref, wg_ref, bg_ref, lam_ref, h0_ref,
         out_ref, hl_ref, uperm, xb_scr, a_scr, b_scr, h_scr, hin_scr) = refs
    step = pl.program_id(2)
    chunk = (nchunks - 1 - step) if reverse else step
    halo = _V7X_SUBLANES
    seg_len = ts // _NSEG
    c = u_ref.shape[1]

    @pl.when(step == 0)
    def _():
        h_scr[...] = h0_ref[...]

    lanes = _V7X_LANES
    ncol = c // lanes
    blocks = [slice(i * lanes, (i + 1) * lanes) for i in range(ncol)]

    pad_lo = (RG_CONV_W - 1) // 2
    pad_hi = RG_CONV_W - 1 - pad_lo
    for cols in blocks:
        perm = _to_segment_major(u_ref[:, cols])
        uperm[pad_lo * _NSEG:pad_lo * _NSEG + ts, cols] = perm
        for i in range(-pad_lo, 0):
            edge = jnp.where(chunk == 0, 0.0, up_ref[halo + i:halo + i + 1, cols])
            src_rows = slice((seg_len + i) * _NSEG, (seg_len + i + 1) * _NSEG)
            uperm[(i + pad_lo) * _NSEG:(i + pad_lo + 1) * _NSEG, cols] = _neighbour_segment_tile(
                perm[src_rows, :], edge, toward_next=False)
        for i in range(seg_len, seg_len + pad_hi):
            edge = jnp.where(chunk == nchunks - 1, 0.0, un_ref[i - seg_len:i - seg_len + 1, cols])
            src_rows = slice((i - seg_len) * _NSEG, (i - seg_len + 1) * _NSEG)
            uperm[(i + pad_lo) * _NSEG:(i + pad_lo + 1) * _NSEG, cols] = _neighbour_segment_tile(
                perm[src_rows, :], edge, toward_next=True)
        cw = [jnp.broadcast_to(cw_ref[k:k + 1, cols], (_NSEG, lanes)) for k in range(RG_CONV_W)]
        cb = jnp.broadcast_to(cb_ref[:, cols], (_NSEG, lanes))
        taps = [uperm[q * _NSEG:(q + 1) * _NSEG, cols] for q in range(seg_len + RG_CONV_W - 1)]
        for j in range(seg_len):
            acc = cb
            for k in range(RG_CONV_W):
                acc = acc + cw[k] * taps[j + k]
            xb_scr[j * _NSEG:(j + 1) * _NSEG, cols] = acc

    xb = xb_scr[...]
    g = _dot(xb.astype(_BF16), wg_ref[...]) + bg_ref[...]
    t_r = jnp.tanh(g[:, :c])
    t_i = jnp.tanh(g[:, c:])
    lam = lam_ref[...]
    softplus_neg_lam = jnp.maximum(-lam, 0.0) + jnp.log1p(jnp.exp(-jnp.abs(lam)))
    c0 = (-0.5 * RG_C) * softplus_neg_lam
    log_a = c0 * t_r + c0
    a = jnp.exp(log_a)
    z = jnp.tanh(-log_a) * (1.0 + a * a)
    root = z * lax.rsqrt(jnp.maximum(z, _SQRT_FLOOR))
    a_scr[...] = a
    b_scr[...] = root * ((0.5 * t_i + 0.5) * xb)

    h = [jnp.zeros((_NSEG, lanes), _F32)] * ncol
    p = [jnp.ones((_NSEG, lanes), _F32)] * ncol
    order = range(seg_len - 1, -1, -1) if reverse else range(seg_len)
    for j in order:
        rows = slice(j * _NSEG, (j + 1) * _NSEG)
        for i, cols in enumerate(blocks):
            aj = a_scr[rows, cols]
            h[i] = aj * h[i] + b_scr[rows, cols]
            p[i] = aj * p[i]
            b_scr[rows, cols] = h[i]
            a_scr[rows, cols] = p[i]
    j_last = 0 if reverse else seg_len - 1

    carry = h_scr[...]
    for m in (range(_NSEG - 1, -1, -1) if reverse else range(_NSEG)):
        hin_scr[m:m + 1, :] = carry
        row = j_last * _NSEG + m
        carry = b_scr[row:row + 1, :] + a_scr[row:row + 1, :] * carry
    h_scr[...] = carry
    hl_ref[...] = carry

    for cols in blocks:
        local = b_scr[:, cols].reshape(seg_len, _NSEG, lanes)
        prod = a_scr[:, cols].reshape(seg_len, _NSEG, lanes)
        states = (local + prod * hin_scr[:, cols][None]).reshape(ts, lanes)
        if final:
            nat_scr[:, cols] = _from_segment_major(states + hf_ref[:, cols])
        else:
            out_ref[:, cols] = states
    if final:
        out_ref[...] = (nat_scr[...] * gate_ref[...].astype(_F32)).astype(_BF16)


def _rg_scan(u, conv_w, conv_b, w_gates, b_gates, lam, h0, reverse, hf=None, gate=None):
    bsz, s, r = u.shape
    c = _RG_PAIR
    ts = min(s, 512)
    nchunks = s // ts
    halo = _V7X_SUBLANES
    final = hf is not None

    def chunk_of(t):
        return (nchunks - 1 - t) if reverse else t

    main = pl.BlockSpec((None, ts, c), lambda b, j, t: (b, chunk_of(t), j))
    prev = pl.BlockSpec((None, halo, c),
                        lambda b, j, t: (b, jnp.maximum(chunk_of(t) * (ts // halo) - 1, 0), j))
    nxt = pl.BlockSpec((None, halo, c),
                       lambda b, j, t: (b, jnp.minimum((chunk_of(t) + 1) * (ts // halo), s // halo - 1), j))
    vec = pl.BlockSpec((1, c), lambda b, j, t: (0, j))
    state = pl.BlockSpec((None, 1, c), lambda b, j, t: (b, 0, j))
    in_specs = [main, prev, nxt,
                pl.BlockSpec((RG_CONV_W, c), lambda b, j, t: (0, j)),
                vec,
                pl.BlockSpec((None, c, 2 * c), lambda b, j, t: (j, 0, 0)),
                pl.BlockSpec((None, 1, 2 * c), lambda b, j, t: (j, 0, 0)),
                vec, state]
    args = [u, u, u, conv_w, conv_b, w_gates, b_gates, lam, h0]
    scratch = [pltpu.VMEM((ts + (RG_CONV_W - 1) * _NSEG, c), _F32), pltpu.VMEM((ts, c), _F32),
               pltpu.VMEM((ts, c), _F32), pltpu.VMEM((ts, c), _F32),
               pltpu.VMEM((1, c), _F32), pltpu.VMEM((_NSEG, c), _F32)]
    if final:
        in_specs += [main, main]
        args += [hf, gate]
        scratch.append(pltpu.VMEM((ts, c), _F32))
    out_dtype = _BF16 if final else _F32
    return pl.pallas_call(
        functools.partial(_rg_scan_body, reverse=reverse, final=final, ts=ts, nchunks=nchunks),
        grid=(bsz, r // c, nchunks),
        in_specs=in_specs,
        out_specs=[main, state],
        out_shape=[jax.ShapeDtypeStruct((bsz, s, r), out_dtype),
                   jax.ShapeDtypeStruct((bsz, 1, r), _F32)],
        scratch_shapes=scratch,
        compiler_params=_params("parallel", "parallel", "arbitrary"),
        name="rg_scan_%s%s" % ("bwd" if reverse else "fwd", "_final" if final else ""),
    )(*args)


def _mm_res_ln_body(a_ref, w_ref, bias_ref, x_ref, m_ref, g_ref, b_ref, o_ref, *, gate_row, alpha, chunk):
    for r0 in range(0, a_ref.shape[0], chunk):
        rows = slice(r0, r0 + chunk)
        y = _dot(a_ref[rows, :].astype(_BF16), w_ref[...]) + bias_ref[...]
        v = alpha * x_ref[rows, :] + m_ref[gate_row:gate_row + 1, :] * y
        o_ref[rows, :] = _layer_norm(v, g_ref[...], b_ref[...])


def _mm_res_ln(a, w, bias, x, mods, layer, gate_row, ln_g, ln_b, alpha, name):
    bsz, s, d = x.shape
    k = a.shape[-1]
    tm = 512
    return pl.pallas_call(
        functools.partial(_mm_res_ln_body, gate_row=gate_row, alpha=alpha, chunk=256),
        grid=(bsz, s // tm),
        in_specs=[pl.BlockSpec((None, tm, k), lambda b, i: (b, i, 0)),
                  pl.BlockSpec((k, d), lambda b, i: (0, 0)),
                  pl.BlockSpec((1, d), lambda b, i: (0, 0)),
                  pl.BlockSpec((None, tm, d), lambda b, i: (b, i, 0)),
                  _mod_spec(layer, d),
                  _vec_spec(2 * layer, d), _vec_spec(2 * layer, d)],
        out_specs=pl.BlockSpec((None, tm, d), lambda b, i: (b, i, 0)),
        out_shape=jax.ShapeDtypeStruct((bsz, s, d), _F32),
        compiler_params=_params("parallel", "parallel"),
        name=name,
    )(a, w, bias, x, mods, ln_g, ln_b)


def _ffn_body(x_ref, m_ref, w1_ref, w3_ref, w2_ref, g_ref, b_ref, o_ref, h_scr, *, nf, alpha, edge_rows):
    f = pl.program_id(2)
    tm = x_ref.shape[0]

    def step(first, last, chunk):
        for r0 in range(0, tm, chunk):
            rows = slice(r0, r0 + chunk)
            if first:
                h_scr[rows, :] = (x_ref[rows, :] * (1.0 + m_ref[4:5, :]) + m_ref[3:4, :]).astype(_BF16)
            h = h_scr[rows, :]
            g = _dot(h, w1_ref[...])
            u = _dot(h, w3_ref[...])
            act = (g * jax.nn.sigmoid(g) * u).astype(_BF16)
            if first:
                o_ref[rows, :] = _dot(act, w2_ref[...])
            else:
                o_ref[rows, :] += _dot(act, w2_ref[...])
            if last:
                v = alpha * x_ref[rows, :] + m_ref[5:6, :] * o_ref[rows, :]
                o_ref[rows, :] = _layer_norm(v, g_ref[...], b_ref[...])

    @pl.when(f == 0)
    def _():
        step(True, False, edge_rows)

    @pl.when(jnp.logical_and(f > 0, f < nf - 1))
    def _():
        step(False, False, tm)

    @pl.when(f == nf - 1)
    def _():
        step(False, True, edge_rows)


def _ffn(x, mods, layer, w1, w3, w2, ln_g, ln_b, alpha):
    bsz, s, d = x.shape
    dff = w1.shape[2]
    tm, tf = 1024, 512
    nf = dff // tf
    xspec = pl.BlockSpec((None, tm, d), lambda b, i, f: (b, i, 0))
    return pl.pallas_call(
        functools.partial(_ffn_body, nf=nf, alpha=alpha, edge_rows=256),
        grid=(bsz, s // tm, nf),
        in_specs=[xspec,
                  _mod_spec(layer, d),
                  pl.BlockSpec((None, d, tf), lambda b, i, f: (layer, 0, f)),
                  pl.BlockSpec((None, d, tf), lambda b, i, f: (layer, 0, f)),
                  pl.BlockSpec((None, tf, d), lambda b, i, f: (layer, f, 0)),
                  _vec_spec(2 * layer + 1, d), _vec_spec(2 * layer + 1, d)],
        out_specs=xspec,
        out_shape=jax.ShapeDtypeStruct((bsz, s, d), _F32),
        scratch_shapes=[pltpu.VMEM((tm, d), _BF16)],
        compiler_params=_params("parallel", "parallel", "arbitrary", vmem=_VMEM_LIMIT_FFN),
        name="ffn",
    )(x, mods, w1, w3, w2, ln_g, ln_b)


def _pool_body(x_ref, xp_ref, xn_ref, m_ref, w_ref, pb_ref, ps_ref, g_ref, b_ref, o_ref, hext, y_scr,
               *, tm, nchunks, seq, alpha):
    i = pl.program_id(1)
    halo = _V7X_SUBLANES
    scale = 1.0 + m_ref[1:2, :]
    shift = m_ref[0:1, :]
    hext[0:halo, :] = jnp.where(i == 0, 0.0, xp_ref[...] * scale + shift)
    hext[halo:halo + tm, :] = x_ref[...] * scale + shift
    hext[halo + tm:2 * halo + tm, :] = jnp.where(i == nchunks - 1, 0.0, xn_ref[...] * scale + shift)
    t = i * tm + lax.broadcasted_iota(jnp.int32, (tm, 1), 0)
    gd = x_ref.shape[1] // len(POOL_WINDOWS)
    for g, win in enumerate(POOL_WINDOWS):
        cols = slice(g * gd, (g + 1) * gd)
        lo = jnp.maximum(t - win // 2, 0)
        hi = jnp.minimum(t - win // 2 + win, seq)
        inv = 1.0 / (hi - lo).astype(_F32)
        acc = hext[pl.ds(halo - win // 2, tm), cols]
        for j in range(1, win):
            acc = acc + hext[pl.ds(halo - win // 2 + j, tm), cols]
        p = acc * inv - hext[halo:halo + tm, cols]
        y = _dot(p.astype(_BF16), w_ref[g]) + pb_ref[:, cols]
        y_scr[:, cols] = y * ps_ref[:, cols]
    v = alpha * x_ref[...] + m_ref[2:3, :] * y_scr[...]
    o_ref[...] = _layer_norm(v, g_ref[...], b_ref[...])


def _pool_layer(x, mods, layer, pool_w, pool_b, pool_scale, ln_g, ln_b, alpha):
    bsz, s, d = x.shape
    tm = 512
    nchunks = s // tm
    halo = _V7X_SUBLANES
    row = pl.BlockSpec((1, d), lambda b, i: (0, 0))
    xspec = pl.BlockSpec((None, tm, d), lambda b, i: (b, i, 0))
    return pl.pallas_call(
        functools.partial(_pool_body, tm=tm, nchunks=nchunks, seq=s, alpha=alpha),
        grid=(bsz, nchunks),
        in_specs=[xspec,
                  pl.BlockSpec((None, halo, d), lambda b, i: (b, jnp.maximum(i * (tm // halo) - 1, 0), 0)),
                  pl.BlockSpec((None, halo, d),
                               lambda b, i: (b, jnp.minimum((i + 1) * (tm // halo), s // halo - 1), 0)),
                  _mod_spec(layer, d),
                  pl.BlockSpec(pool_w.shape, lambda b, i: (0, 0, 0)),
                  row, row,
                  _vec_spec(2 * layer, d), _vec_spec(2 * layer, d)],
        out_specs=xspec,
        out_shape=jax.ShapeDtypeStruct((bsz, s, d), _F32),
        scratch_shapes=[pltpu.VMEM((tm + 2 * halo, d), _F32), pltpu.VMEM((tm, d), _F32)],
        compiler_params=_params("parallel", "parallel"),
        name="pool_mixer",
    )(x, x, x, mods, pool_w, pool_b, pool_scale, ln_g, ln_b)


def _conf_in_body(x_ref, m_ref, wa_ref, wb_ref, ba_ref, bb_ref, o_ref, h_scr):
    @pl.when(pl.program_id(2) == 0)
    def _():
        h_scr[...] = (x_ref[...] * (1.0 + m_ref[1:2, :]) + m_ref[0:1, :]).astype(_BF16)

    h = h_scr[...]
    ua = _dot(h, wa_ref[...]) + ba_ref[...]
    ub = _dot(h, wb_ref[...]) + bb_ref[...]
    o_ref[...] = ua * jax.nn.sigmoid(ub)


def _conf_in(x, mods, layer, w1, b1):
    bsz, s, d = x.shape
    tm, tn = 512, 1024
    nj = d // tn
    return pl.pallas_call(
        _conf_in_body,
        grid=(bsz, s // tm, nj),
        in_specs=[pl.BlockSpec((None, tm, d), lambda b, i, j: (b, i, 0)),
                  _mod_spec(layer, d),
                  pl.BlockSpec((d, tn), lambda b, i, j: (0, j)),
                  pl.BlockSpec((d, tn), lambda b, i, j: (0, j + nj)),
                  pl.BlockSpec((1, tn), lambda b, i, j: (0, j)),
                  pl.BlockSpec((1, tn), lambda b, i, j: (0, j + nj))],
        out_specs=pl.BlockSpec((None, tm, tn), lambda b, i, j: (b, i, j)),
        out_shape=jax.ShapeDtypeStruct((bsz, s, d), _F32),
        scratch_shapes=[pltpu.VMEM((tm, d), _BF16)],
        compiler_params=_params("parallel", "parallel", "arbitrary"),
        name="conf_in",
    )(x, mods, w1, w1, b1, b1)


_CONF_HALO = 16


def _conf_out_body(u_ref, up_ref, un_ref, x_ref, m_ref, dw_ref, cg_ref, cb_ref, w2_ref, b2_ref,
                   g_ref, b_ref, o_ref, uslab, pslab, nslab, uperm, conv_scr, *, tm, nchunks, alpha):
    i = pl.program_id(1)
    halo = _CONF_HALO
    lanes = _V7X_LANES
    d = x_ref.shape[1]
    ncol = d // lanes
    seg_len = tm // _NSEG
    pad = (CONF_KERNEL - 1) // 2
    first = i == 0
    last = i == nchunks - 1
    for cb_ in range(ncol):
        cols = slice(cb_ * lanes, (cb_ + 1) * lanes)
        uslab[cb_] = u_ref[:, cols]
        pslab[cb_] = jnp.where(first, 0.0, up_ref[:, cols])
        nslab[cb_] = jnp.where(last, 0.0, un_ref[:, cols])

    def lane_block(cb_, carry):
        perm = _to_segment_major(uslab[cb_])
        uperm[pad * _NSEG:pad * _NSEG + tm, :] = perm
        for t in range(-pad, 0):
            src_rows = slice((seg_len + t) * _NSEG, (seg_len + t + 1) * _NSEG)
            uperm[(t + pad) * _NSEG:(t + pad + 1) * _NSEG, :] = _neighbour_segment_tile(
                perm[src_rows, :], pslab[cb_, halo + t:halo + t + 1, :], toward_next=False)
        for t in range(seg_len, seg_len + pad):
            src_rows = slice((t - seg_len) * _NSEG, (t - seg_len + 1) * _NSEG)
            uperm[(t + pad) * _NSEG:(t + pad + 1) * _NSEG, :] = _neighbour_segment_tile(
                perm[src_rows, :], nslab[cb_, t - seg_len:t - seg_len + 1, :], toward_next=True)
        w = [jnp.broadcast_to(dw_ref[cb_, k:k + 1, :], (_NSEG, lanes)) for k in range(CONF_KERNEL)]
        bias = jnp.broadcast_to(dw_ref[cb_, CONF_KERNEL:CONF_KERNEL + 1, :], (_NSEG, lanes))
        for j in range(0, seg_len, 2):
            acc0 = bias
            acc1 = bias
            for q in range(j, j + CONF_KERNEL + 1):
                tap = uperm[q * _NSEG:(q + 1) * _NSEG, :]
                if q - j < CONF_KERNEL:
                    acc0 = acc0 + w[q - j] * tap
                if q - j >= 1:
                    acc1 = acc1 + w[q - j - 1] * tap
            conv_scr[cb_, j * _NSEG:(j + 1) * _NSEG, :] = acc0
            conv_scr[cb_, (j + 1) * _NSEG:(j + 2) * _NSEG, :] = acc1
        return carry

    lax.fori_loop(0, ncol, lane_block, 0)

    cv = jnp.concatenate([conv_scr[cb_] for cb_ in range(ncol)], axis=-1)
    u2 = _layer_norm(cv, cg_ref[...], cb_ref[...])
    u2 = u2 * jax.nn.sigmoid(u2)
    y = _from_segment_major(_dot(u2.astype(_BF16), w2_ref[...]) + b2_ref[...])
    v = alpha * x_ref[...] + m_ref[2:3, :] * y
    o_ref[...] = _layer_norm(v, g_ref[...], b_ref[...])


def _conf_out(u, x, mods, layer, dw_blocks, cg, cb, w2, b2, ln_g, ln_b, alpha):
    bsz, s, d = x.shape
    tm = 256
    nchunks = s // tm
    halo = _CONF_HALO
    ncol = d // _V7X_LANES
    row = pl.BlockSpec((1, d), lambda b, i: (0, 0))
    xspec = pl.BlockSpec((None, tm, d), lambda b, i: (b, i, 0))
    return pl.pallas_call(
        functools.partial(_conf_out_body, tm=tm, nchunks=nchunks, alpha=alpha),
        grid=(bsz, nchunks),
        in_specs=[xspec,
                  pl.BlockSpec((None, halo, d), lambda b, i: (b, jnp.maximum(i * (tm // halo) - 1, 0), 0)),
                  pl.BlockSpec((None, halo, d),
                               lambda b, i: (b, jnp.minimum((i + 1) * (tm // halo), s // halo - 1), 0)),
                  xspec,
                  _mod_spec(layer, d),
                  pl.BlockSpec(dw_blocks.shape, lambda b, i: (0, 0, 0)),
                  row, row,
                  pl.BlockSpec((d, d), lambda b, i: (0, 0)),
                  row,
                  _vec_spec(2 * layer, d), _vec_spec(2 * layer, d)],
        out_specs=xspec,
        out_shape=jax.ShapeDtypeStruct((bsz, s, d), _F32),
        scratch_shapes=[pltpu.VMEM((ncol, tm, _V7X_LANES), _F32),
                        pltpu.VMEM((ncol, halo, _V7X_LANES), _F32),
                        pltpu.VMEM((ncol, halo, _V7X_LANES), _F32),
                        pltpu.VMEM((tm + (CONF_KERNEL - 1) * _NSEG, _V7X_LANES), _F32),
                        pltpu.VMEM((ncol, tm, _V7X_LANES), _F32)],
        compiler_params=_params("parallel", "parallel"),
        name="conf_out",
    )(u, u, u, x, mods, dw_blocks, cg, cb, w2, b2, ln_g, ln_b)


def _ft_in_body(x_ref, m_ref, w_ref, zr_ref, zi_ref, *, groups):
    h = (x_ref[...] * (1.0 + m_ref[1:2, :]) + m_ref[0:1, :]).astype(_BF16)
    gd = x_ref.shape[1] // groups
    for g in range(groups):
        cols = slice(g * gd, (g + 1) * gd)
        z = _dot(h[:, cols], w_ref[...])
        zr_ref[:, cols] = z[:, :gd]
        zi_ref[:, cols] = z[:, gd:]


def _ft_in(x, mods, layer, w_dft):
    bsz, s, d = x.shape
    tm = 512
    xspec = pl.BlockSpec((None, tm, d), lambda b, i: (b, i, 0))
    return pl.pallas_call(
        functools.partial(_ft_in_body, groups=FT_GROUPS),
        grid=(bsz, s // tm),
        in_specs=[xspec, _mod_spec(layer, d),
                  pl.BlockSpec(w_dft.shape, lambda b, i: (0, 0))],
        out_specs=[xspec, xspec],
        out_shape=[jax.ShapeDtypeStruct((bsz, s, d), _F32)] * 2,
        compiler_params=_params("parallel", "parallel"),
        name="ft_channel_dft",
    )(x, mods, w_dft)


def _swap_major(v, n_major):
    rows, c = v.shape
    n_minor = rows // n_major
    return pltpu.einshape("abc->bac", v.reshape(n_major, n_minor, c)).reshape(rows, c)


def _ft_seq_body(zr_ref, zi_ref, ma_ref, w2_ref, o_ref, pr_scr, pi_scr, yr_scr, yi_scr, *, n1, n2):
    pr_scr[...] = _swap_major(zr_ref[...], n1)
    pi_scr[...] = _swap_major(zi_ref[...], n1)
    for s2 in range(n2):
        rows = slice(s2 * n1, (s2 + 1) * n1)
        zc = jnp.concatenate([pr_scr[rows, :], pi_scr[rows, :]], axis=0).astype(_BF16)
        y = _dot(ma_ref[s2], zc)
        yr_scr[rows, :] = y[:n1]
        yi_scr[rows, :] = y[n1:]
    pr_scr[...] = _swap_major(yr_scr[...], n2)
    pi_scr[...] = _swap_major(yi_scr[...], n2)
    for k1 in range(n1):
        rows = slice(k1 * n2, (k1 + 1) * n2)
        yc = jnp.concatenate([pr_scr[rows, :], pi_scr[rows, :]], axis=0).astype(_BF16)
        yr_scr[rows, :] = _dot(w2_ref[...], yc)
    o_ref[...] = _swap_major(yr_scr[...], n1)


def _ft_seq(zr, zi, ma, w2cat, n1, n2):
    bsz, s, d = zr.shape
    tc = _V7X_LANES
    spec = pl.BlockSpec((None, s, tc), lambda b, j: (b, 0, j))
    return pl.pallas_call(
        functools.partial(_ft_seq_body, n1=n1, n2=n2),
        grid=(bsz, d // tc),
        in_specs=[spec, spec,
                  pl.BlockSpec(ma.shape, lambda b, j: (0, 0, 0)),
                  pl.BlockSpec(w2cat.shape, lambda b, j: (0, 0))],
        out_specs=spec,
        out_shape=jax.ShapeDtypeStruct((bsz, s, d), _F32),
        scratch_shapes=[pltpu.VMEM((s, tc), _F32)] * 4,
        compiler_params=_params("parallel", "parallel"),
        name="ft_sequence_dft",
    )(zr, zi, ma, w2cat)


def _ft_tables(s, gd):
    n2 = 128
    n1 = s // n2
    p = np.arange(gd)
    ang = 2.0 * np.pi * ((p[:, None] * p[None, :]) % gd) / gd
    scale = 1.0 / np.sqrt(float(s) * gd)
    w_dft = np.concatenate([np.cos(ang), -np.sin(ang)], axis=1) * scale
    k1 = np.arange(n1)
    s1 = np.arange(n1)
    s2 = np.arange(n2)
    pos = n2 * s1[None, None, :] + s2[:, None, None]
    ang = 2.0 * np.pi * ((k1[None, :, None] * pos) % s) / s
    er, ei = np.cos(ang), np.sin(ang)
    ma = np.concatenate([np.concatenate([er, ei], axis=2),
                         np.concatenate([-ei, er], axis=2)], axis=1)
    k2 = np.arange(n2)
    ang = 2.0 * np.pi * ((k2[:, None] * s2[None, :]) % n2) / n2
    w2cat = np.concatenate([np.cos(ang), np.sin(ang)], axis=1)
    as_bf16 = lambda a: jnp.asarray(a.astype(np.float32)).astype(_BF16)
    return as_bf16(w_dft), as_bf16(ma), as_bf16(w2cat), n1, n2


def _pad_blocks_last(a):
    lead = a.shape[:-1]
    blk = a.shape[-1] // RG_BLOCKS
    a = a.reshape(lead + (RG_BLOCKS, blk))
    a = jnp.pad(a, [(0, 0)] * len(lead) + [(0, 0), (0, _RG_BLOCK_PAD - blk)])
    return a.reshape(lead + (RG_BLOCKS * _RG_BLOCK_PAD,))


def _pair_block_diag(w):
    blk = w.shape[-1]
    w = jnp.pad(w, [(0, 0), (0, _RG_BLOCK_PAD - blk), (0, _RG_BLOCK_PAD - blk)])
    w = w.reshape(RG_BLOCKS // 2, 2, _RG_BLOCK_PAD, _RG_BLOCK_PAD)
    zero = jnp.zeros_like(w[:, 0])
    top = jnp.concatenate([w[:, 0], zero], axis=-1)
    bottom = jnp.concatenate([zero, w[:, 1]], axis=-1)
    return jnp.concatenate([top, bottom], axis=1)


def _pos_tables(rows, cols, dim):
    quarter = dim // 4
    omega = 1.0 / (POS_THETA ** (jnp.arange(quarter, dtype=_F32) / quarter))
    ar = jnp.arange(rows, dtype=_F32)[:, None] * omega[None]
    ac = jnp.arange(cols, dtype=_F32)[:, None] * omega[None]
    er = jnp.concatenate([jnp.sin(ar), jnp.cos(ar)], axis=-1)
    ec = jnp.concatenate([jnp.sin(ac), jnp.cos(ac)], axis=-1)
    return er, ec


def _rglru_layer(x, ctx, mods, layer, bsz, w_gate, w_x, conv_w, conv_b, wr, br, wi, bi, lam, w_out,
                 ln_g, ln_b, alpha):
    s, d = x.shape[1], x.shape[2]
    w_x_p = _pad_blocks_last(w_x).astype(_BF16)
    w_gate_p = _pad_blocks_last(w_gate).astype(_BF16)
    conv_w_p = _pad_blocks_last(conv_w)
    conv_b_p = _pad_blocks_last(conv_b)[None]
    blk = w_out.shape[0] // RG_BLOCKS
    w_out_p = jnp.pad(w_out.reshape(RG_BLOCKS, blk, d), [(0, 0), (0, _RG_BLOCK_PAD - blk), (0, 0)])
    w_out_p = w_out_p.reshape(RG_BLOCKS * _RG_BLOCK_PAD, d).astype(_BF16)
    npairs = RG_BLOCKS // 2
    er_ec = _pos_tables(s // GRID_W, GRID_W, d)
    x0, xw, gate = _rg_in(x, mods, layer, lambda b: b, w_x_p, w_gate_p, er_ec)
    xwc = _rg_in(ctx, mods, layer, lambda b: bsz, w_x_p, None, None)
    zeros_state = jnp.zeros((bsz, 1, xw.shape[-1]), _F32)
    hf = None
    out = None
    for direction in range(2):
        reverse = direction == 1
        w_gates = (0.5 * jnp.concatenate([_pair_block_diag(wr[direction]), _pair_block_diag(wi[direction])],
                                         axis=-1)).astype(_BF16)
        b_gates = 0.5 * jnp.concatenate([_pad_blocks_last(br[direction]).reshape(npairs, 1, _RG_PAIR),
                                         _pad_blocks_last(bi[direction]).reshape(npairs, 1, _RG_PAIR)], axis=-1)
        lam_p = _pad_blocks_last(lam[direction])[None]
        _, h0 = _rg_scan(xwc, conv_w_p, conv_b_p, w_gates, b_gates, lam_p, zeros_state, reverse)
        if not reverse:
            hf, _ = _rg_scan(xw, conv_w_p, conv_b_p, w_gates, b_gates, lam_p, h0, reverse)
        else:
            out, _ = _rg_scan(xw, conv_w_p, conv_b_p, w_gates, b_gates, lam_p, h0, reverse, hf, gate)
    zero_bias = jnp.zeros((1, d), _F32)
    return _mm_res_ln(out, w_out_p, zero_bias, x0, mods, layer, 2, ln_g, ln_b, alpha, "rg_out")


def kernel(x, c, ctx, c_ctx, mod_w, mod_b, ln_g, ln_b, ffn_w1, ffn_w3, ffn_w2, rg_w_gate, rg_w_x, rg_conv_w, rg_conv_b, rg_wr, rg_br, rg_wi, rg_bi, rg_lam, rg_w_out, pool_w, pool_b, pool_scale, cv_w1, cv_b1, cv_dw, cv_dwb, cv_ln_g, cv_ln_b, cv_w2, cv_b2, ft_w, ft_b):
    bsz, s, d = x.shape
    depth = mod_w.shape[0]
    alpha = float((2 * depth) ** 0.25)
    assert depth == N_MIXERS and bsz < _MOD_ROWS, "one layer per mixer kind; the context feeds layer 0 only"

    cond = jnp.zeros((_MOD_ROWS, d), _F32).at[:bsz].set(c).at[bsz].set(c_ctx)
    mods = _cond_vectors(cond, mod_w, mod_b).reshape(depth, _MOD_ROWS, 6, d)
    ln_g3 = ln_g.reshape(depth * 2, 1, d)
    ln_b3 = ln_b.reshape(depth * 2, 1, d)
    w1_all = ffn_w1.astype(_BF16)
    w3_all = ffn_w3.astype(_BF16)
    w2_all = ffn_w2.astype(_BF16)

    def row(v):
        return v.reshape(1, -1)

    def ffn(xc, layer):
        return _ffn(xc, mods, layer, w1_all, w3_all, w2_all, ln_g3, ln_b3, alpha)

    xc = _rglru_layer(x, ctx, mods, 0, bsz, rg_w_gate[0], rg_w_x[0], rg_conv_w[0], rg_conv_b[0],
                      rg_wr[0], rg_br[0], rg_wi[0], rg_bi[0], rg_lam[0], rg_w_out[0], ln_g3, ln_b3, alpha)
    xc = ffn(xc, 0)

    xc = _pool_layer(xc, mods, 1, pool_w[0].astype(_BF16), row(pool_b[0]), row(pool_scale[0]),
                     ln_g3, ln_b3, alpha)
    xc = ffn(xc, 1)

    u = _conf_in(xc, mods, 2, cv_w1[0].astype(_BF16), row(cv_b1[0]))
    ncol = d // _V7X_LANES
    dw_blocks = jnp.concatenate([cv_dw[0], cv_dwb[0][None]], axis=0)
    dw_blocks = dw_blocks.reshape(CONF_KERNEL + 1, ncol, _V7X_LANES).transpose(1, 0, 2)
    xc = _conf_out(u, xc, mods, 2, dw_blocks, row(cv_ln_g[0]), row(cv_ln_b[0]),
                   cv_w2[0].astype(_BF16), row(cv_b2[0]), ln_g3, ln_b3, alpha)
    xc = ffn(xc, 2)

    w_dft, ma, w2cat, n1, n2 = _ft_tables(s, d // FT_GROUPS)
    zr, zi = _ft_in(xc, mods, 3, w_dft)
    f = _ft_seq(zr, zi, ma, w2cat, n1, n2)
    xc = _mm_res_ln(f, ft_w[0].astype(_BF16), row(ft_b[0]), xc, mods, 3, 2, ln_g3, ln_b3, alpha, "ft_out")
    xc = ffn(xc, 3)
    return xc
```

```python
import functools

import numpy as np
import jax
import jax.numpy as jnp
from jax import lax
from jax.experimental import pallas as pl
from jax.experimental.pallas import tpu as pltpu

_F32 = jnp.float32
_BF16 = jnp.bfloat16

GRID_W = 64
N_MIXERS = 4
RG_BLOCKS = 16
RG_CONV_W = 4
RG_C = 8.0
POOL_WINDOWS = (2, 4, 8, 16)
CONF_KERNEL = 31
FT_GROUPS = 4
LN_EPS = 1e-5
POS_THETA = 10000.0

_V7X_LANES = 128
_V7X_SUBLANES = 8
_V7X_VMEM_BYTES = 64 * 1024 * 1024
_VMEM_LIMIT = _V7X_VMEM_BYTES - 12 * 1024 * 1024
_VMEM_LIMIT_FFN = _V7X_VMEM_BYTES - 4 * 1024 * 1024

_RG_BLOCK_PAD = 192
_RG_PAIR = 2 * _RG_BLOCK_PAD

_MOD_ROWS = 8
_NSEG = _V7X_SUBLANES
_SQRT_FLOOR = 1e-30


def _params(*sem, vmem=_VMEM_LIMIT):
    return pltpu.CompilerParams(dimension_semantics=sem, vmem_limit_bytes=vmem)


def _layer_norm(v, g, b):
    mu = jnp.mean(v, axis=-1, keepdims=True)
    xc = v - mu
    var = jnp.mean(xc * xc, axis=-1, keepdims=True)
    return xc * lax.rsqrt(var + LN_EPS) * g + b


def _dot(a, b):
    return jnp.dot(a, b, preferred_element_type=_F32)


def _to_segment_major(v):
    rows, c = v.shape
    return pltpu.einshape("mjc->jmc", v.reshape(_NSEG, rows // _NSEG, c)).reshape(rows, c)


def _from_segment_major(v):
    rows, c = v.shape
    return pltpu.einshape("jmc->mjc", v.reshape(rows // _NSEG, _NSEG, c)).reshape(rows, c)


def _neighbour_segment_tile(tile, edge_row, toward_next):
    sub = lax.broadcasted_iota(jnp.int32, tile.shape, 0)
    if toward_next:
        return jnp.where(sub == _NSEG - 1, edge_row, pltpu.roll(tile, shift=_NSEG - 1, axis=0))
    return jnp.where(sub == 0, edge_row, pltpu.roll(tile, shift=1, axis=0))


def _mod_spec(layer, d, row_of_batch=lambda b: b):
    return pl.BlockSpec((None, None, 6, d), lambda b, *_: (layer, row_of_batch(b), 0, 0))


def _vec_spec(index, d):
    return pl.BlockSpec((None, 1, d), lambda *_: (index, 0, 0))


def _cond_body(c_ref, w_ref, b_ref, o_ref):
    cond = c_ref[...]
    cond = cond * jax.nn.sigmoid(cond)
    o_ref[...] = _dot(cond.astype(_BF16), w_ref[...].astype(_BF16)) + b_ref[...]


def _cond_vectors(cond, mod_w, mod_b):
    depth, d, n = mod_w.shape
    tn = 1024
    return pl.pallas_call(
        _cond_body,
        grid=(depth, n // tn),
        in_specs=[pl.BlockSpec((_MOD_ROWS, d), lambda l, j: (0, 0)),
                  pl.BlockSpec((None, d, tn), lambda l, j: (l, 0, j)),
                  pl.BlockSpec((None, 1, tn), lambda l, j: (l, 0, j))],
        out_specs=pl.BlockSpec((None, _MOD_ROWS, tn), lambda l, j: (l, 0, j)),
        out_shape=jax.ShapeDtypeStruct((depth, _MOD_ROWS, n), _F32),
        compiler_params=_params("parallel", "parallel"),
        name="cond_vectors",
    )(cond, mod_w, mod_b.reshape(depth, 1, n))


def _rg_in_body(*refs, latent, tm, first_chunk):
    if latent:
        x_ref, er_ref, ec_ref, m_ref, wx_ref, wg_ref, x0_ref, xw_ref, gate_ref, h_scr = refs
    else:
        x_ref, m_ref, wx_ref, xw_ref, h_scr = refs

    def step(first, chunk):
        for r0 in range(0, tm, chunk):
            rows = slice(r0, r0 + chunk)
            if first:
                if latent:
                    half = er_ref.shape[1]
                    for r in range(r0 // GRID_W, (r0 + chunk) // GRID_W):
                        grid_row = slice(r * GRID_W, (r + 1) * GRID_W)
                        x0_ref[grid_row, :half] = x_ref[grid_row, :half] + er_ref[r:r + 1, :]
                        x0_ref[grid_row, half:] = x_ref[grid_row, half:] + ec_ref[...]
                    x0 = x0_ref[rows, :]
                else:
                    x0 = x_ref[rows, :]
                h_scr[rows, :] = (x0 * (1.0 + m_ref[1:2, :]) + m_ref[0:1, :]).astype(_BF16)
            h = h_scr[rows, :]
            xw_ref[rows, :] = _dot(h, wx_ref[...])
            if latent:
                gate_ref[rows, :] = jax.nn.gelu(_dot(h, wg_ref[...])).astype(_BF16)

    @pl.when(pl.program_id(2) == 0)
    def _():
        step(True, first_chunk)

    @pl.when(pl.program_id(2) > 0)
    def _():
        step(False, tm)


def _rg_in(x, mods, layer, mod_row, w_x, w_gate, pos_tables):
    bsz, s, d = x.shape
    r = w_x.shape[1]
    latent = pos_tables is not None
    tm = 512 if latent else s
    tn = 1024
    grid = (bsz, s // tm, r // tn)
    xspec = pl.BlockSpec((None, tm, d), lambda b, i, j: (b, i, 0))
    mspec = _mod_spec(layer, d, mod_row)
    wspec = pl.BlockSpec((d, tn), lambda b, i, j: (0, j))
    ospec = pl.BlockSpec((None, tm, tn), lambda b, i, j: (b, i, j))
    body = functools.partial(_rg_in_body, latent=latent, tm=tm, first_chunk=min(tm, 256))
    scratch = [pltpu.VMEM((tm, d), _BF16)]
    if latent:
        er, ec = pos_tables
        rows_per_tile = tm // GRID_W
        return pl.pallas_call(
            body, grid=grid,
            in_specs=[xspec,
                      pl.BlockSpec((rows_per_tile, er.shape[1]), lambda b, i, j: (i, 0)),
                      pl.BlockSpec(ec.shape, lambda b, i, j: (0, 0)),
                      mspec, wspec, wspec],
            out_specs=[xspec, ospec, ospec],
            out_shape=[jax.ShapeDtypeStruct((bsz, s, d), _F32),
                       jax.ShapeDtypeStruct((bsz, s, r), _F32),
                       jax.ShapeDtypeStruct((bsz, s, r), _BF16)],
            scratch_shapes=scratch,
            compiler_params=_params("parallel", "parallel", "arbitrary"),
            name="rg_in_latent",
        )(x, er, ec, mods, w_x, w_gate)
    return pl.pallas_call(
        body, grid=grid,
        in_specs=[xspec, mspec, wspec],
        out_specs=ospec,
        out_shape=jax.ShapeDtypeStruct((bsz, s, r), _F32),
        scratch_shapes=scratch,
        compiler_params=_params("parallel", "parallel", "arbitrary"),
        name="rg_in_context",
    )(x, mods, w_x)


def _rg_scan_body(*refs, reverse, mode, ts, nchunks):
    final = mode == "final"
    if final:
        (xb_ref, wg_ref, bg_ref, lam_ref, h0_ref, hf_ref, gate_ref,
         out_ref, hl_ref, a_scr, b_scr, h_scr, hin_scr, nat_scr) = refs
    elif mode == "first":
        (u_ref, up_ref, un_ref, cw_ref, cb_ref, wg_ref, bg_ref, lam_ref, h0_ref,
         out_ref, hl_ref, xb_ref, uperm, a_scr, b_scr, h_scr, hin_scr) = refs
    else:
        (u_ref, up_ref, un_ref, cw_ref, cb_ref, wg_ref, bg_ref, lam_ref, h0_ref,
         out_ref, hl_ref, uperm, xb_ref, a_scr, b_scr, h_scr, hin_scr) = refs
    step = pl.program_id(2)
    chunk = (nchunks - 1 - step) if reverse else step
    halo = _V7X_SUBLANES
    seg_len = ts // _NSEG
    c = xb_ref.shape[1]

    @pl.when(step == 0)
    def _():
        h_scr[...] = h0_ref[...]

    lanes = _V7X_LANES
    ncol = c // lanes
    blocks = [slice(i * lanes, (i + 1) * lanes) for i in range(ncol)]

    pad_lo = (RG_CONV_W - 1) // 2
    pad_hi = RG_CONV_W - 1 - pad_lo
    for cols in ([] if final else blocks):
        perm = _to_segment_major(u_ref[:, cols])
        uperm[pad_lo * _NSEG:pad_lo * _NSEG + ts, cols] = perm
        for i in range(-pad_lo, 0):
            edge = jnp.where(chunk == 0, 0.0, up_ref[halo + i:halo + i + 1, cols])
            src_rows = slice((seg_len + i) * _NSEG, (seg_len + i + 1) * _NSEG)
            uperm[(i + pad_lo) * _NSEG:(i + pad_lo + 1) * _NSEG, cols] = _neighbour_segment_tile(
                perm[src_rows, :], edge, toward_next=False)
        for i in range(seg_len, seg_len + pad_hi):
            edge = jnp.where(chunk == nchunks - 1, 0.0, un_ref[i - seg_len:i - seg_len + 1, cols])
            src_rows = slice((i - seg_len) * _NSEG, (i - seg_len + 1) * _NSEG)
            uperm[(i + pad_lo) * _NSEG:(i + pad_lo + 1) * _NSEG, cols] = _neighbour_segment_tile(
                perm[src_rows, :], edge, toward_next=True)
        cw = [jnp.broadcast_to(cw_ref[k:k + 1, cols], (_NSEG, lanes)) for k in range(RG_CONV_W)]
        cb = jnp.broadcast_to(cb_ref[:, cols], (_NSEG, lanes))
        taps = [uperm[q * _NSEG:(q + 1) * _NSEG, cols] for q in range(seg_len + RG_CONV_W - 1)]
        for j in range(seg_len):
            acc = cb
            for k in range(RG_CONV_W):
                acc = acc + cw[k] * taps[j + k]
            xb_ref[j * _NSEG:(j + 1) * _NSEG, cols] = acc

    xb = xb_ref[...]
    g = _dot(xb.astype(_BF16), wg_ref[...]) + bg_ref[...]
    t_r = jnp.tanh(g[:, :c])
    t_i = jnp.tanh(g[:, c:])
    lam = lam_ref[...]
    softplus_neg_lam = jnp.maximum(-lam, 0.0) + jnp.log1p(jnp.exp(-jnp.abs(lam)))
    c0 = (-0.5 * RG_C) * softplus_neg_lam
    log_a = c0 * t_r + c0
    a = jnp.exp(log_a)
    z = jnp.tanh(-log_a) * (1.0 + a * a)
    root = z * lax.rsqrt(jnp.maximum(z, _SQRT_FLOOR))
    a_scr[...] = a
    b_scr[...] = root * ((0.5 * t_i + 0.5) * xb)

    h = [jnp.zeros((_NSEG, lanes), _F32)] * ncol
    p = [jnp.ones((_NSEG, lanes), _F32)] * ncol
    order = range(seg_len - 1, -1, -1) if reverse else range(seg_len)
    for j in order:
        rows = slice(j * _NSEG, (j + 1) * _NSEG)
        for i, cols in enumerate(blocks):
            aj = a_scr[rows, cols]
            h[i] = aj * h[i] + b_scr[rows, cols]
            p[i] = aj * p[i]
            b_scr[rows, cols] = h[i]
            a_scr[rows, cols] = p[i]
    j_last = 0 if reverse else seg_len - 1

    carry = h_scr[...]
    for m in (range(_NSEG - 1, -1, -1) if reverse else range(_NSEG)):
        hin_scr[m:m + 1, :] = carry
        row = j_last * _NSEG + m
        carry = b_scr[row:row + 1, :] + a_scr[row:row + 1, :] * carry
    h_scr[...] = carry
    hl_ref[...] = carry

    for cols in blocks:
        local = b_scr[:, cols].reshape(seg_len, _NSEG, lanes)
        prod = a_scr[:, cols].reshape(seg_len, _NSEG, lanes)
        states = (local + prod * hin_scr[:, cols][None]).reshape(ts, lanes)
        if final:
            nat_scr[:, cols] = _from_segment_major(states + hf_ref[:, cols])
        else:
            out_ref[:, cols] = states
    if final:
        out_ref[...] = (nat_scr[...] * gate_ref[...].astype(_F32)).astype(_BF16)


def _rg_scan(u, conv_w, conv_b, w_gates, b_gates, lam, h0, reverse, mode="plain", hf=None, gate=None):
    bsz, s, r = u.shape
    c = _RG_PAIR
    ts = min(s, 512)
    nchunks = s // ts
    halo = _V7X_SUBLANES
    final = mode == "final"

    def chunk_of(t):
        return (nchunks - 1 - t) if reverse else t

    main = pl.BlockSpec((None, ts, c), lambda b, j, t: (b, chunk_of(t), j))
    prev = pl.BlockSpec((None, halo, c),
                        lambda b, j, t: (b, jnp.maximum(chunk_of(t) * (ts // halo) - 1, 0), j))
    nxt = pl.BlockSpec((None, halo, c),
                       lambda b, j, t: (b, jnp.minimum((chunk_of(t) + 1) * (ts // halo), s // halo - 1), j))
    vec = pl.BlockSpec((1, c), lambda b, j, t: (0, j))
    state = pl.BlockSpec((None, 1, c), lambda b, j, t: (b, 0, j))
    gate_specs = [pl.BlockSpec((None, c, 2 * c), lambda b, j, t: (j, 0, 0)),
                  pl.BlockSpec((None, 1, 2 * c), lambda b, j, t: (j, 0, 0)),
                  vec, state]
    tile = pltpu.VMEM((ts, c), _F32)
    carries = [pltpu.VMEM((1, c), _F32), pltpu.VMEM((_NSEG, c), _F32)]
    out_specs = [main, state]
    out_shape = [jax.ShapeDtypeStruct((bsz, s, r), _BF16 if final else _F32),
                 jax.ShapeDtypeStruct((bsz, 1, r), _F32)]
    if final:
        in_specs = [main] + gate_specs + [main, main]
        args = [u, w_gates, b_gates, lam, h0, hf, gate]
        scratch = [tile, tile] + carries + [tile]
    else:
        in_specs = [main, prev, nxt, pl.BlockSpec((RG_CONV_W, c), lambda b, j, t: (0, j)), vec] + gate_specs
        args = [u, u, u, conv_w, conv_b, w_gates, b_gates, lam, h0]
        conv_in = pltpu.VMEM((ts + (RG_CONV_W - 1) * _NSEG, c), _F32)
        if mode == "first":
            out_specs.append(main)
            out_shape.append(jax.ShapeDtypeStruct((bsz, s, r), _F32))
            scratch = [conv_in, tile, tile] + carries
        else:
            scratch = [conv_in, tile, tile, tile] + carries
    return pl.pallas_call(
        functools.partial(_rg_scan_body, reverse=reverse, mode=mode, ts=ts, nchunks=nchunks),
        grid=(bsz, r // c, nchunks),
        in_specs=in_specs,
        out_specs=out_specs,
        out_shape=out_shape,
        scratch_shapes=scratch,
        compiler_params=_params("parallel", "parallel", "arbitrary"),
        name="rg_scan_%s_%s" % ("bwd" if reverse else "fwd", mode),
    )(*args)


def _mm_res_ln_body(a_ref, w_ref, bias_ref, x_ref, m_ref, g_ref, b_ref, o_ref, *, gate_row, alpha, chunk):
    for r0 in range(0, a_ref.shape[0], chunk):
        rows = slice(r0, r0 + chunk)
        y = _dot(a_ref[rows, :].astype(_BF16), w_ref[...]) + bias_ref[...]
        v = alpha * x_ref[rows, :] + m_ref[gate_row:gate_row + 1, :] * y
        o_ref[rows, :] = _layer_norm(v, g_ref[...], b_ref[...])


def _mm_res_ln(a, w, bias, x, mods, layer, gate_row, ln_g, ln_b, alpha, name):
    bsz, s, d = x.shape
    k = a.shape[-1]
    tm = 512
    return pl.pallas_call(
        functools.partial(_mm_res_ln_body, gate_row=gate_row, alpha=alpha, chunk=256),
        grid=(bsz, s // tm),
        in_specs=[pl.BlockSpec((None, tm, k), lambda b, i: (b, i, 0)),
                  pl.BlockSpec((k, d), lambda b, i: (0, 0)),
                  pl.BlockSpec((1, d), lambda b, i: (0, 0)),
                  pl.BlockSpec((None, tm, d), lambda b, i: (b, i, 0)),
                  _mod_spec(layer, d),
                  _vec_spec(2 * layer, d), _vec_spec(2 * layer, d)],
        out_specs=pl.BlockSpec((None, tm, d), lambda b, i: (b, i, 0)),
        out_shape=jax.ShapeDtypeStruct((bsz, s, d), _F32),
        compiler_params=_params("parallel", "parallel"),
        name=name,
    )(a, w, bias, x, mods, ln_g, ln_b)


def _ffn_body(x_ref, m_ref, w1_ref, w3_ref, w2_ref, g_ref, b_ref, o_ref, h_scr, *, nf, alpha, edge_rows):
    f = pl.program_id(2)
    tm = x_ref.shape[0]

    def step(first, last, chunk):
        for r0 in range(0, tm, chunk):
            rows = slice(r0, r0 + chunk)
            if first:
                h_scr[rows, :] = (x_ref[rows, :] * (1.0 + m_ref[4:5, :]) + m_ref[3:4, :]).astype(_BF16)
            h = h_scr[rows, :]
            g = _dot(h, w1_ref[...])
            u = _dot(h, w3_ref[...])
            act = (g * jax.nn.sigmoid(g) * u).astype(_BF16)
            if first:
                o_ref[rows, :] = _dot(act, w2_ref[...])
            else:
                o_ref[rows, :] += _dot(act, w2_ref[...])
            if last:
                v = alpha * x_ref[rows, :] + m_ref[5:6, :] * o_ref[rows, :]
                o_ref[rows, :] = _layer_norm(v, g_ref[...], b_ref[...])

    @pl.when(f == 0)
    def _():
        step(True, False, edge_rows)

    @pl.when(jnp.logical_and(f > 0, f < nf - 1))
    def _():
        step(False, False, tm)

    @pl.when(f == nf - 1)
    def _():
        step(False, True, edge_rows)


def _ffn(x, mods, layer, w1, w3, w2, ln_g, ln_b, alpha):
    bsz, s, d = x.shape
    dff = w1.shape[2]
    tm, tf = 1024, 512
    nf = dff // tf
    xspec = pl.BlockSpec((None, tm, d), lambda b, i, f: (b, i, 0))
    return pl.pallas_call(
        functools.partial(_ffn_body, nf=nf, alpha=alpha, edge_rows=256),
        grid=(bsz, s // tm, nf),
        in_specs=[xspec,
                  _mod_spec(layer, d),
                  pl.BlockSpec((None, d, tf), lambda b, i, f: (layer, 0, f)),
                  pl.BlockSpec((None, d, tf), lambda b, i, f: (layer, 0, f)),
                  pl.BlockSpec((None, tf, d), lambda b, i, f: (layer, f, 0)),
                  _vec_spec(2 * layer + 1, d), _vec_spec(2 * layer + 1, d)],
        out_specs=xspec,
        out_shape=jax.ShapeDtypeStruct((bsz, s, d), _F32),
        scratch_shapes=[pltpu.VMEM((tm, d), _BF16)],
        compiler_params=_params("parallel", "parallel", "arbitrary", vmem=_VMEM_LIMIT_FFN),
        name="ffn",
    )(x, mods, w1, w3, w2, ln_g, ln_b)


def _pool_body(x_ref, xp_ref, xn_ref, m_ref, w_ref, pb_ref, ps_ref, g_ref, b_ref, o_ref, hext, y_scr,
               *, tm, nchunks, seq, alpha):
    i = pl.program_id(1)
    halo = _V7X_SUBLANES
    scale = 1.0 + m_ref[1:2, :]
    shift = m_ref[0:1, :]
    hext[0:halo, :] = jnp.where(i == 0, 0.0, xp_ref[...] * scale + shift)
    hext[halo:halo + tm, :] = x_ref[...] * scale + shift
    hext[halo + tm:2 * halo + tm, :] = jnp.where(i == nchunks - 1, 0.0, xn_ref[...] * scale + shift)
    t = i * tm + lax.broadcasted_iota(jnp.int32, (tm, 1), 0)
    gd = x_ref.shape[1] // len(POOL_WINDOWS)
    for g, win in enumerate(POOL_WINDOWS):
        cols = slice(g * gd, (g + 1) * gd)
        lo = jnp.maximum(t - win // 2, 0)
        hi = jnp.minimum(t - win // 2 + win, seq)
        inv = 1.0 / (hi - lo).astype(_F32)
        acc = hext[pl.ds(halo - win // 2, tm), cols]
        for j in range(1, win):
            acc = acc + hext[pl.ds(halo - win // 2 + j, tm), cols]
        p = acc * inv - hext[halo:halo + tm, cols]
        y = _dot(p.astype(_BF16), w_ref[g]) + pb_ref[:, cols]
        y_scr[:, cols] = y * ps_ref[:, cols]
    v = alpha * x_ref[...] + m_ref[2:3, :] * y_scr[...]
    o_ref[...] = _layer_norm(v, g_ref[...], b_ref[...])


def _pool_layer(x, mods, layer, pool_w, pool_b, pool_scale, ln_g, ln_b, alpha):
    bsz, s, d = x.shape
    tm = 512
    nchunks = s // tm
    halo = _V7X_SUBLANES
    row = pl.BlockSpec((1, d), lambda b, i: (0, 0))
    xspec = pl.BlockSpec((None, tm, d), lambda b, i: (b, i, 0))
    return pl.pallas_call(
        functools.partial(_pool_body, tm=tm, nchunks=nchunks, seq=s, alpha=alpha),
        grid=(bsz, nchunks),
        in_specs=[xspec,
                  pl.BlockSpec((None, halo, d), lambda b, i: (b, jnp.maximum(i * (tm // halo) - 1, 0), 0)),
                  pl.BlockSpec((None, halo, d),
                               lambda b, i: (b, jnp.minimum((i + 1) * (tm // halo), s // halo - 1), 0)),
                  _mod_spec(layer, d),
                  pl.BlockSpec(pool_w.shape, lambda b, i: (0, 0, 0)),
                  row, row,
                  _vec_spec(2 * layer, d), _vec_spec(2 * layer, d)],
        out_specs=xspec,
        out_shape=jax.ShapeDtypeStruct((bsz, s, d), _F32),
        scratch_shapes=[pltpu.VMEM((tm + 2 * halo, d), _F32), pltpu.VMEM((tm, d), _F32)],
        compiler_params=_params("parallel", "parallel"),
        name="pool_mixer",
    )(x, x, x, mods, pool_w, pool_b, pool_scale, ln_g, ln_b)


def _conf_in_body(x_ref, m_ref, wa_ref, wb_ref, ba_ref, bb_ref, o_ref, h_scr, *, first_chunk):
    tm = x_ref.shape[0]

    def step(first, chunk):
        for r0 in range(0, tm, chunk):
            rows = slice(r0, r0 + chunk)
            if first:
                h_scr[rows, :] = (x_ref[rows, :] * (1.0 + m_ref[1:2, :]) + m_ref[0:1, :]).astype(_BF16)
            h = h_scr[rows, :]
            ua = _dot(h, wa_ref[...]) + ba_ref[...]
            ub = _dot(h, wb_ref[...]) + bb_ref[...]
            o_ref[rows, :] = ua * jax.nn.sigmoid(ub)

    @pl.when(pl.program_id(2) == 0)
    def _():
        step(True, first_chunk)

    @pl.when(pl.program_id(2) > 0)
    def _():
        step(False, tm)


def _conf_in(x, mods, layer, w1, b1):
    bsz, s, d = x.shape
    tm, tn = 512, 1024
    nj = d // tn
    return pl.pallas_call(
        functools.partial(_conf_in_body, first_chunk=256),
        grid=(bsz, s // tm, nj),
        in_specs=[pl.BlockSpec((None, tm, d), lambda b, i, j: (b, i, 0)),
                  _mod_spec(layer, d),
                  pl.BlockSpec((d, tn), lambda b, i, j: (0, j)),
                  pl.BlockSpec((d, tn), lambda b, i, j: (0, j + nj)),
                  pl.BlockSpec((1, tn), lambda b, i, j: (0, j)),
                  pl.BlockSpec((1, tn), lambda b, i, j: (0, j + nj))],
        out_specs=pl.BlockSpec((None, tm, tn), lambda b, i, j: (b, i, j)),
        out_shape=jax.ShapeDtypeStruct((bsz, s, d), _F32),
        scratch_shapes=[pltpu.VMEM((tm, d), _BF16)],
        compiler_params=_params("parallel", "parallel", "arbitrary"),
        name="conf_in",
    )(x, mods, w1, w1, b1, b1)


_CONF_HALO = 16


def _conf_out_body(u_ref, up_ref, un_ref, x_ref, m_ref, dw_ref, cg_ref, cb_ref, w2_ref, b2_ref,
                   g_ref, b_ref, o_ref, uslab, pslab, nslab, uperm, conv_scr, *, tm, nchunks, alpha):
    i = pl.program_id(1)
    halo = _CONF_HALO
    lanes = _V7X_LANES
    d = x_ref.shape[1]
    ncol = d // lanes
    seg_len = tm // _NSEG
    pad = (CONF_KERNEL - 1) // 2
    first = i == 0
    last = i == nchunks - 1
    for cb_ in range(ncol):
        cols = slice(cb_ * lanes, (cb_ + 1) * lanes)
        uslab[cb_] = u_ref[:, cols]
        pslab[cb_] = jnp.where(first, 0.0, up_ref[:, cols])
        nslab[cb_] = jnp.where(last, 0.0, un_ref[:, cols])

    def lane_block(cb_, carry):
        perm = _to_segment_major(uslab[cb_])
        uperm[pad * _NSEG:pad * _NSEG + tm, :] = perm
        for t in range(-pad, 0):
            src_rows = slice((seg_len + t) * _NSEG, (seg_len + t + 1) * _NSEG)
            uperm[(t + pad) * _NSEG:(t + pad + 1) * _NSEG, :] = _neighbour_segment_tile(
                perm[src_rows, :], pslab[cb_, halo + t:halo + t + 1, :], toward_next=False)
        for t in range(seg_len, seg_len + pad):
            src_rows = slice((t - seg_len) * _NSEG, (t - seg_len + 1) * _NSEG)
            uperm[(t + pad) * _NSEG:(t + pad + 1) * _NSEG, :] = _neighbour_segment_tile(
                perm[src_rows, :], nslab[cb_, t - seg_len:t - seg_len + 1, :], toward_next=True)
        w = [jnp.broadcast_to(dw_ref[cb_, k:k + 1, :], (_NSEG, lanes)) for k in range(CONF_KERNEL)]
        bias = jnp.broadcast_to(dw_ref[cb_, CONF_KERNEL:CONF_KERNEL + 1, :], (_NSEG, lanes))
        for j in range(0, seg_len, 2):
            acc0 = bias
            acc1 = bias
            for q in range(j, j + CONF_KERNEL + 1):
                tap = uperm[q * _NSEG:(q + 1) * _NSEG, :]
                if q - j < CONF_KERNEL:
                    acc0 = acc0 + w[q - j] * tap
                if q - j >= 1:
                    acc1 = acc1 + w[q - j - 1] * tap
            conv_scr[cb_, j * _NSEG:(j + 1) * _NSEG, :] = acc0
            conv_scr[cb_, (j + 1) * _NSEG:(j + 2) * _NSEG, :] = acc1
        return carry

    lax.fori_loop(0, ncol, lane_block, 0)

    cv = jnp.concatenate([conv_scr[cb_] for cb_ in range(ncol)], axis=-1)
    u2 = _layer_norm(cv, cg_ref[...], cb_ref[...])
    u2 = u2 * jax.nn.sigmoid(u2)
    y = _from_segment_major(_dot(u2.astype(_BF16), w2_ref[...]) + b2_ref[...])
    v = alpha * x_ref[...] + m_ref[2:3, :] * y
    o_ref[...] = _layer_norm(v, g_ref[...], b_ref[...])


def _conf_out(u, x, mods, layer, dw_blocks, cg, cb, w2, b2, ln_g, ln_b, alpha):
    bsz, s, d = x.shape
    tm = 256
    nchunks = s // tm
    halo = _CONF_HALO
    ncol = d // _V7X_LANES
    row = pl.BlockSpec((1, d), lambda b, i: (0, 0))
    xspec = pl.BlockSpec((None, tm, d), lambda b, i: (b, i, 0))
    return pl.pallas_call(
        functools.partial(_conf_out_body, tm=tm, nchunks=nchunks, alpha=alpha),
        grid=(bsz, nchunks),
        in_specs=[xspec,
                  pl.BlockSpec((None, halo, d), lambda b, i: (b, jnp.maximum(i * (tm // halo) - 1, 0), 0)),
                  pl.BlockSpec((None, halo, d),
                               lambda b, i: (b, jnp.minimum((i + 1) * (tm // halo), s // halo - 1), 0)),
                  xspec,
                  _mod_spec(layer, d),
                  pl.BlockSpec(dw_blocks.shape, lambda b, i: (0, 0, 0)),
                  row, row,
                  pl.BlockSpec((d, d), lambda b, i: (0, 0)),
                  row,
                  _vec_spec(2 * layer, d), _vec_spec(2 * layer, d)],
        out_specs=xspec,
        out_shape=jax.ShapeDtypeStruct((bsz, s, d), _F32),
        scratch_shapes=[pltpu.VMEM((ncol, tm, _V7X_LANES), _F32),
                        pltpu.VMEM((ncol, halo, _V7X_LANES), _F32),
                        pltpu.VMEM((ncol, halo, _V7X_LANES), _F32),
                        pltpu.VMEM((tm + (CONF_KERNEL - 1) * _NSEG, _V7X_LANES), _F32),
                        pltpu.VMEM((ncol, tm, _V7X_LANES), _F32)],
        compiler_params=_params("parallel", "parallel"),
        name="conf_out",
    )(u, u, u, x, mods, dw_blocks, cg, cb, w2, b2, ln_g, ln_b)


def _ft_in_body(x_ref, m_ref, w_ref, zr_ref, zi_ref, *, groups):
    h = (x_ref[...] * (1.0 + m_ref[1:2, :]) + m_ref[0:1, :]).astype(_BF16)
    gd = x_ref.shape[1] // groups
    for g in range(groups):
        cols = slice(g * gd, (g + 1) * gd)
        z = _dot(h[:, cols], w_ref[...])
        zr_ref[:, cols] = z[:, :gd]
        zi_ref[:, cols] = z[:, gd:]


def _ft_in(x, mods, layer, w_dft):
    bsz, s, d = x.shape
    tm = 512
    xspec = pl.BlockSpec((None, tm, d), lambda b, i: (b, i, 0))
    return pl.pallas_call(
        functools.partial(_ft_in_body, groups=FT_GROUPS),
        grid=(bsz, s // tm),
        in_specs=[xspec, _mod_spec(layer, d),
                  pl.BlockSpec(w_dft.shape, lambda b, i: (0, 0))],
        out_specs=[xspec, xspec],
        out_shape=[jax.ShapeDtypeStruct((bsz, s, d), _F32)] * 2,
        compiler_params=_params("parallel", "parallel"),
        name="ft_channel_dft",
    )(x, mods, w_dft)


def _swap_major(v, n_major):
    rows, c = v.shape
    n_minor = rows // n_major
    return pltpu.einshape("abc->bac", v.reshape(n_major, n_minor, c)).reshape(rows, c)


def _ft_seq_body(zr_ref, zi_ref, ma_ref, w2_ref, o_ref, pr_scr, pi_scr, yr_scr, yi_scr, *, n1, n2):
    pr_scr[...] = _swap_major(zr_ref[...], n1)
    pi_scr[...] = _swap_major(zi_ref[...], n1)
    for s2 in range(n2):
        rows = slice(s2 * n1, (s2 + 1) * n1)
        zc = jnp.concatenate([pr_scr[rows, :], pi_scr[rows, :]], axis=0).astype(_BF16)
        y = _dot(ma_ref[s2], zc)
        yr_scr[rows, :] = y[:n1]
        yi_scr[rows, :] = y[n1:]
    pr_scr[...] = _swap_major(yr_scr[...], n2)
    pi_scr[...] = _swap_major(yi_scr[...], n2)
    for k1 in range(n1):
        rows = slice(k1 * n2, (k1 + 1) * n2)
        yc = jnp.concatenate([pr_scr[rows, :], pi_scr[rows, :]], axis=0).astype(_BF16)
        yr_scr[rows, :] = _dot(w2_ref[...], yc)
    o_ref[...] = _swap_major(yr_scr[...], n1)


def _ft_seq(zr, zi, ma, w2cat, n1, n2):
    bsz, s, d = zr.shape
    tc = _V7X_LANES
    spec = pl.BlockSpec((None, s, tc), lambda b, j: (b, 0, j))
    return pl.pallas_call(
        functools.partial(_ft_seq_body, n1=n1, n2=n2),
        grid=(bsz, d // tc),
        in_specs=[spec, spec,
                  pl.BlockSpec(ma.shape, lambda b, j: (0, 0, 0)),
                  pl.BlockSpec(w2cat.shape, lambda b, j: (0, 0))],
        out_specs=spec,
        out_shape=jax.ShapeDtypeStruct((bsz, s, d), _F32),
        scratch_shapes=[pltpu.VMEM((s, tc), _F32)] * 4,
        compiler_params=_params("parallel", "parallel"),
        name="ft_sequence_dft",
    )(zr, zi, ma, w2cat)


def _ft_tables(s, gd):
    n2 = 128
    n1 = s // n2
    p = np.arange(gd)
    ang = 2.0 * np.pi * ((p[:, None] * p[None, :]) % gd) / gd
    scale = 1.0 / np.sqrt(float(s) * gd)
    w_dft = np.concatenate([np.cos(ang), -np.sin(ang)], axis=1) * scale
    k1 = np.arange(n1)
    s1 = np.arange(n1)
    s2 = np.arange(n2)
    pos = n2 * s1[None, None, :] + s2[:, None, None]
    ang = 2.0 * np.pi * ((k1[None, :, None] * pos) % s) / s
    er, ei = np.cos(ang), np.sin(ang)
    ma = np.concatenate([np.concatenate([er, ei], axis=2),
                         np.concatenate([-ei, er], axis=2)], axis=1)
    k2 = np.arange(n2)
    ang = 2.0 * np.pi * ((k2[:, None] * s2[None, :]) % n2) / n2
    w2cat = np.concatenate([np.cos(ang), np.sin(ang)], axis=1)
    as_bf16 = lambda a: jnp.asarray(a.astype(np.float32)).astype(_BF16)
    return as_bf16(w_dft), as_bf16(ma), as_bf16(w2cat), n1, n2


def _pad_blocks_last(a):
    lead = a.shape[:-1]
    blk = a.shape[-1] // RG_BLOCKS
    a = a.reshape(lead + (RG_BLOCKS, blk))
    a = jnp.pad(a, [(0, 0)] * len(lead) + [(0, 0), (0, _RG_BLOCK_PAD - blk)])
    return a.reshape(lead + (RG_BLOCKS * _RG_BLOCK_PAD,))


def _pair_block_diag(w):
    blk = w.shape[-1]
    w = jnp.pad(w, [(0, 0), (0, _RG_BLOCK_PAD - blk), (0, _RG_BLOCK_PAD - blk)])
    w = w.reshape(RG_BLOCKS // 2, 2, _RG_BLOCK_PAD, _RG_BLOCK_PAD)
    zero = jnp.zeros_like(w[:, 0])
    top = jnp.concatenate([w[:, 0], zero], axis=-1)
    bottom = jnp.concatenate([zero, w[:, 1]], axis=-1)
    return jnp.concatenate([top, bottom], axis=1)


def _pos_tables(rows, cols, dim):
    quarter = dim // 4
    omega = 1.0 / (POS_THETA ** (jnp.arange(quarter, dtype=_F32) / quarter))
    ar = jnp.arange(rows, dtype=_F32)[:, None] * omega[None]
    ac = jnp.arange(cols, dtype=_F32)[:, None] * omega[None]
    er = jnp.concatenate([jnp.sin(ar), jnp.cos(ar)], axis=-1)
    ec = jnp.concatenate([jnp.sin(ac), jnp.cos(ac)], axis=-1)
    return er, ec


def _rglru_layer(x, ctx, mods, layer, bsz, w_gate, w_x, conv_w, conv_b, wr, br, wi, bi, lam, w_out,
                 ln_g, ln_b, alpha):
    s, d = x.shape[1], x.shape[2]
    w_x_p = _pad_blocks_last(w_x).astype(_BF16)
    w_gate_p = _pad_blocks_last(w_gate).astype(_BF16)
    conv_w_p = _pad_blocks_last(conv_w)
    conv_b_p = _pad_blocks_last(conv_b)[None]
    blk = w_out.shape[0] // RG_BLOCKS
    w_out_p = jnp.pad(w_out.reshape(RG_BLOCKS, blk, d), [(0, 0), (0, _RG_BLOCK_PAD - blk), (0, 0)])
    w_out_p = w_out_p.reshape(RG_BLOCKS * _RG_BLOCK_PAD, d).astype(_BF16)
    npairs = RG_BLOCKS // 2
    er_ec = _pos_tables(s // GRID_W, GRID_W, d)
    x0, xw, gate = _rg_in(x, mods, layer, lambda b: b, w_x_p, w_gate_p, er_ec)
    xwc = _rg_in(ctx, mods, layer, lambda b: bsz, w_x_p, None, None)
    zeros_state = jnp.zeros((bsz, 1, xw.shape[-1]), _F32)
    hf = None
    out = None
    for direction in range(2):
        reverse = direction == 1
        w_gates = (0.5 * jnp.concatenate([_pair_block_diag(wr[direction]), _pair_block_diag(wi[direction])],
                                         axis=-1)).astype(_BF16)
        b_gates = 0.5 * jnp.concatenate([_pad_blocks_last(br[direction]).reshape(npairs, 1, _RG_PAIR),
                                         _pad_blocks_last(bi[direction]).reshape(npairs, 1, _RG_PAIR)], axis=-1)
        lam_p = _pad_blocks_last(lam[direction])[None]
        _, h0 = _rg_scan(xwc, conv_w_p, conv_b_p, w_gates, b_gates, lam_p, zeros_state, reverse)
        if not reverse:
            hf, _, xb = _rg_scan(xw, conv_w_p, conv_b_p, w_gates, b_gates, lam_p, h0, reverse, "first")
        else:
            out, _ = _rg_scan(xb, None, None, w_gates, b_gates, lam_p, h0, reverse, "final", hf, gate)
    zero_bias = jnp.zeros((1, d), _F32)
    return _mm_res_ln(out, w_out_p, zero_bias, x0, mods, layer, 2, ln_g, ln_b, alpha, "rg_out")


def kernel(x, c, ctx, c_ctx, mod_w, mod_b, ln_g, ln_b, ffn_w1, ffn_w3, ffn_w2, rg_w_gate, rg_w_x, rg_conv_w, rg_conv_b, rg_wr, rg_br, rg_wi, rg_bi, rg_lam, rg_w_out, pool_w, pool_b, pool_scale, cv_w1, cv_b1, cv_dw, cv_dwb, cv_ln_g, cv_ln_b, cv_w2, cv_b2, ft_w, ft_b):
    bsz, s, d = x.shape
    depth = mod_w.shape[0]
    alpha = float((2 * depth) ** 0.25)
    assert depth == N_MIXERS and bsz < _MOD_ROWS, "one layer per mixer kind; the context feeds layer 0 only"

    cond = jnp.zeros((_MOD_ROWS, d), _F32).at[:bsz].set(c).at[bsz].set(c_ctx)
    mods = _cond_vectors(cond, mod_w, mod_b).reshape(depth, _MOD_ROWS, 6, d)
    ln_g3 = ln_g.reshape(depth * 2, 1, d)
    ln_b3 = ln_b.reshape(depth * 2, 1, d)
    w1_all = ffn_w1.astype(_BF16)
    w3_all = ffn_w3.astype(_BF16)
    w2_all = ffn_w2.astype(_BF16)

    def row(v):
        return v.reshape(1, -1)

    def ffn(xc, layer):
        return _ffn(xc, mods, layer, w1_all, w3_all, w2_all, ln_g3, ln_b3, alpha)

    xc = _rglru_layer(x, ctx, mods, 0, bsz, rg_w_gate[0], rg_w_x[0], rg_conv_w[0], rg_conv_b[0],
                      rg_wr[0], rg_br[0], rg_wi[0], rg_bi[0], rg_lam[0], rg_w_out[0], ln_g3, ln_b3, alpha)
    xc = ffn(xc, 0)

    xc = _pool_layer(xc, mods, 1, pool_w[0].astype(_BF16), row(pool_b[0]), row(pool_scale[0]),
                     ln_g3, ln_b3, alpha)
    xc = ffn(xc, 1)

    u = _conf_in(xc, mods, 2, cv_w1[0].astype(_BF16), row(cv_b1[0]))
    ncol = d // _V7X_LANES
    dw_blocks = jnp.concatenate([cv_dw[0], cv_dwb[0][None]], axis=0)
    dw_blocks = dw_blocks.reshape(CONF_KERNEL + 1, ncol, _V7X_LANES).transpose(1, 0, 2)
    xc = _conf_out(u, xc, mods, 2, dw_blocks, row(cv_ln_g[0]), row(cv_ln_b[0]),
                   cv_w2[0].astype(_BF16), row(cv_b2[0]), ln_g3, ln_b3, alpha)
    xc = ffn(xc, 2)

    w_dft, ma, w2cat, n1, n2 = _ft_tables(s, d // FT_GROUPS)
    zr, zi = _ft_in(xc, mods, 3, w_dft)
    f = _ft_seq(zr, zi, ma, w2cat, n1, n2)
    xc = _mm_res_ln(f, ft_w[0].astype(_BF16), row(ft_b[0]), xc, mods, 3, 2, ln_g3, ln_b3, alpha, "ft_out")
    xc = ffn(xc, 3)
    return xc
```

```python
import functools

import numpy as np
import jax
import jax.numpy as jnp
from jax import lax
from jax.experimental import pallas as pl
from jax.experimental.pallas import tpu as pltpu

_F32 = jnp.float32
_BF16 = jnp.bfloat16

GRID_W = 64
N_MIXERS = 4
RG_BLOCKS = 16
RG_CONV_W = 4
RG_C = 8.0
POOL_WINDOWS = (2, 4, 8, 16)
CONF_KERNEL = 31
FT_GROUPS = 4
LN_EPS = 1e-5
POS_THETA = 10000.0

_V7X_LANES = 128
_V7X_SUBLANES = 8
_V7X_VMEM_BYTES = 64 * 1024 * 1024
_VMEM_LIMIT = _V7X_VMEM_BYTES - 12 * 1024 * 1024
_VMEM_LIMIT_LARGE = _V7X_VMEM_BYTES - 4 * 1024 * 1024

_RG_BLOCK_PAD = 192
_RG_PAIR = 2 * _RG_BLOCK_PAD

_MOD_ROWS = 8
_NSEG = _V7X_SUBLANES
_SQRT_FLOOR = 1e-30


def _params(*sem, vmem=_VMEM_LIMIT):
    return pltpu.CompilerParams(dimension_semantics=sem, vmem_limit_bytes=vmem)


def _layer_norm(v, g, b):
    mu = jnp.mean(v, axis=-1, keepdims=True)
    xc = v - mu
    var = jnp.mean(xc * xc, axis=-1, keepdims=True)
    return xc * lax.rsqrt(var + LN_EPS) * g + b


def _dot(a, b):
    return jnp.dot(a, b, preferred_element_type=_F32)


def _to_segment_major(v):
    rows, c = v.shape
    return pltpu.einshape("mjc->jmc", v.reshape(_NSEG, rows // _NSEG, c)).reshape(rows, c)


def _from_segment_major(v):
    rows, c = v.shape
    return pltpu.einshape("jmc->mjc", v.reshape(rows // _NSEG, _NSEG, c)).reshape(rows, c)


def _neighbour_segment_tile(tile, edge_row, toward_next):
    sub = lax.broadcasted_iota(jnp.int32, tile.shape, 0)
    if toward_next:
        return jnp.where(sub == _NSEG - 1, edge_row, pltpu.roll(tile, shift=_NSEG - 1, axis=0))
    return jnp.where(sub == 0, edge_row, pltpu.roll(tile, shift=1, axis=0))


def _mod_spec(layer, d, row_of_batch=lambda b: b):
    return pl.BlockSpec((None, None, 6, d), lambda b, *_: (layer, row_of_batch(b), 0, 0))


def _vec_spec(index, d):
    return pl.BlockSpec((None, 1, d), lambda *_: (index, 0, 0))


def _cond_body(c_ref, w_ref, b_ref, o_ref):
    cond = c_ref[...]
    cond = cond * jax.nn.sigmoid(cond)
    o_ref[...] = _dot(cond.astype(_BF16), w_ref[...].astype(_BF16)) + b_ref[...]


def _cond_vectors(cond, mod_w, mod_b):
    depth, d, n = mod_w.shape
    tn = 1024
    return pl.pallas_call(
        _cond_body,
        grid=(depth, n // tn),
        in_specs=[pl.BlockSpec((_MOD_ROWS, d), lambda l, j: (0, 0)),
                  pl.BlockSpec((None, d, tn), lambda l, j: (l, 0, j)),
                  pl.BlockSpec((None, 1, tn), lambda l, j: (l, 0, j))],
        out_specs=pl.BlockSpec((None, _MOD_ROWS, tn), lambda l, j: (l, 0, j)),
        out_shape=jax.ShapeDtypeStruct((depth, _MOD_ROWS, n), _F32),
        compiler_params=_params("parallel", "parallel"),
        name="cond_vectors",
    )(cond, mod_w, mod_b.reshape(depth, 1, n))


def _rg_in_body(*refs, latent, tm, first_chunk):
    if latent:
        x_ref, er_ref, ec_ref, m_ref, wx_ref, wg_ref, x0_ref, xw_ref, gate_ref, h_scr = refs
    else:
        x_ref, m_ref, wx_ref, xw_ref, h_scr = refs

    def step(first, chunk):
        for r0 in range(0, tm, chunk):
            rows = slice(r0, r0 + chunk)
            if first:
                if latent:
                    half = er_ref.shape[1]
                    for r in range(r0 // GRID_W, (r0 + chunk) // GRID_W):
                        grid_row = slice(r * GRID_W, (r + 1) * GRID_W)
                        x0_ref[grid_row, :half] = x_ref[grid_row, :half] + er_ref[r:r + 1, :]
                        x0_ref[grid_row, half:] = x_ref[grid_row, half:] + ec_ref[...]
                    x0 = x0_ref[rows, :]
                else:
                    x0 = x_ref[rows, :]
                h_scr[rows, :] = (x0 * (1.0 + m_ref[1:2, :]) + m_ref[0:1, :]).astype(_BF16)
            h = h_scr[rows, :]
            xw_ref[rows, :] = _dot(h, wx_ref[...])
            if latent:
                gate_ref[rows, :] = jax.nn.gelu(_dot(h, wg_ref[...])).astype(_BF16)

    @pl.when(pl.program_id(2) == 0)
    def _():
        step(True, first_chunk)

    @pl.when(pl.program_id(2) > 0)
    def _():
        step(False, tm)


def _rg_in(x, mods, layer, mod_row, w_x, w_gate, pos_tables):
    bsz, s, d = x.shape
    r = w_x.shape[1]
    latent = pos_tables is not None
    tm = 1024 if latent else s
    tn = 512
    grid = (bsz, s // tm, r // tn)
    xspec = pl.BlockSpec((None, tm, d), lambda b, i, j: (b, i, 0))
    mspec = _mod_spec(layer, d, mod_row)
    wspec = pl.BlockSpec((d, tn), lambda b, i, j: (0, j))
    ospec = pl.BlockSpec((None, tm, tn), lambda b, i, j: (b, i, j))
    body = functools.partial(_rg_in_body, latent=latent, tm=tm, first_chunk=min(tm, 256))
    scratch = [pltpu.VMEM((tm, d), _BF16)]
    if latent:
        er, ec = pos_tables
        rows_per_tile = tm // GRID_W
        return pl.pallas_call(
            body, grid=grid,
            in_specs=[xspec,
                      pl.BlockSpec((rows_per_tile, er.shape[1]), lambda b, i, j: (i, 0)),
                      pl.BlockSpec(ec.shape, lambda b, i, j: (0, 0)),
                      mspec, wspec, wspec],
            out_specs=[xspec, ospec, ospec],
            out_shape=[jax.ShapeDtypeStruct((bsz, s, d), _F32),
                       jax.ShapeDtypeStruct((bsz, s, r), _F32),
                       jax.ShapeDtypeStruct((bsz, s, r), _BF16)],
            scratch_shapes=scratch,
            compiler_params=_params("parallel", "parallel", "arbitrary", vmem=_VMEM_LIMIT_LARGE),
            name="rg_in_latent",
        )(x, er, ec, mods, w_x, w_gate)
    return pl.pallas_call(
        body, grid=grid,
        in_specs=[xspec, mspec, wspec],
        out_specs=ospec,
        out_shape=jax.ShapeDtypeStruct((bsz, s, r), _F32),
        scratch_shapes=scratch,
        compiler_params=_params("parallel", "parallel", "arbitrary"),
        name="rg_in_context",
    )(x, mods, w_x)


def _rg_scan_body(*refs, reverse, mode, ts, nchunks):
    final = mode == "final"
    if final:
        (xb_ref, wg_ref, bg_ref, lam_ref, h0_ref, hf_ref, gate_ref,
         out_ref, hl_ref, a_scr, b_scr, h_scr, hin_scr, nat_scr) = refs
    elif mode == "first":
        (u_ref, up_ref, un_ref, cw_ref, cb_ref, wg_ref, bg_ref, lam_ref, h0_ref,
         out_ref, hl_ref, xb_ref, uperm, a_scr, b_scr, h_scr, hin_scr) = refs
    else:
        (u_ref, up_ref, un_ref, cw_ref, cb_ref, wg_ref, bg_ref, lam_ref, h0_ref,
         out_ref, hl_ref, uperm, xb_ref, a_scr, b_scr, h_scr, hin_scr) = refs
    step = pl.program_id(2)
    chunk = (nchunks - 1 - step) if reverse else step
    halo = _V7X_SUBLANES
    seg_len = ts // _NSEG
    c = xb_ref.shape[1]

    @pl.when(step == 0)
    def _():
        h_scr[...] = h0_ref[...]

    lanes = _V7X_LANES
    ncol = c // lanes
    blocks = [slice(i * lanes, (i + 1) * lanes) for i in range(ncol)]

    pad_lo = (RG_CONV_W - 1) // 2
    pad_hi = RG_CONV_W - 1 - pad_lo
    for cols in ([] if final else blocks):
        perm = _to_segment_major(u_ref[:, cols])
        uperm[pad_lo * _NSEG:pad_lo * _NSEG + ts, cols] = perm
        for i in range(-pad_lo, 0):
            edge = jnp.where(chunk == 0, 0.0, up_ref[halo + i:halo + i + 1, cols])
            src_rows = slice((seg_len + i) * _NSEG, (seg_len + i + 1) * _NSEG)
            uperm[(i + pad_lo) * _NSEG:(i + pad_lo + 1) * _NSEG, cols] = _neighbour_segment_tile(
                perm[src_rows, :], edge, toward_next=False)
        for i in range(seg_len, seg_len + pad_hi):
            edge = jnp.where(chunk == nchunks - 1, 0.0, un_ref[i - seg_len:i - seg_len + 1, cols])
            src_rows = slice((i - seg_len) * _NSEG, (i - seg_len + 1) * _NSEG)
            uperm[(i + pad_lo) * _NSEG:(i + pad_lo + 1) * _NSEG, cols] = _neighbour_segment_tile(
                perm[src_rows, :], edge, toward_next=True)
        cw = [jnp.broadcast_to(cw_ref[k:k + 1, cols], (_NSEG, lanes)) for k in range(RG_CONV_W)]
        cb = jnp.broadcast_to(cb_ref[:, cols], (_NSEG, lanes))
        taps = [uperm[q * _NSEG:(q + 1) * _NSEG, cols] for q in range(seg_len + RG_CONV_W - 1)]
        for j in range(seg_len):
            acc = cb
            for k in range(RG_CONV_W):
                acc = acc + cw[k] * taps[j + k]
            xb_ref[j * _NSEG:(j + 1) * _NSEG, cols] = acc

    xb = xb_ref[...]
    g = _dot(xb.astype(_BF16), wg_ref[...]) + bg_ref[...]
    t_r = jnp.tanh(g[:, :c])
    t_i = jnp.tanh(g[:, c:])
    lam = lam_ref[...]
    softplus_neg_lam = jnp.maximum(-lam, 0.0) + jnp.log1p(jnp.exp(-jnp.abs(lam)))
    c0 = (-0.5 * RG_C) * softplus_neg_lam
    log_a = c0 * t_r + c0
    a = jnp.exp(log_a)
    z = jnp.tanh(-log_a) * (1.0 + a * a)
    root = z * lax.rsqrt(jnp.maximum(z, _SQRT_FLOOR))
    a_scr[...] = a
    b_scr[...] = root * ((0.5 * t_i + 0.5) * xb)

    h = [jnp.zeros((_NSEG, lanes), _F32)] * ncol
    p = [jnp.ones((_NSEG, lanes), _F32)] * ncol
    order = range(seg_len - 1, -1, -1) if reverse else range(seg_len)
    for j in order:
        rows = slice(j * _NSEG, (j + 1) * _NSEG)
        for i, cols in enumerate(blocks):
            aj = a_scr[rows, cols]
            h[i] = aj * h[i] + b_scr[rows, cols]
            p[i] = aj * p[i]
            b_scr[rows, cols] = h[i]
            a_scr[rows, cols] = p[i]
    j_last = 0 if reverse else seg_len - 1

    carry = h_scr[...]
    for m in (range(_NSEG - 1, -1, -1) if reverse else range(_NSEG)):
        hin_scr[m:m + 1, :] = carry
        row = j_last * _NSEG + m
        carry = b_scr[row:row + 1, :] + a_scr[row:row + 1, :] * carry
    h_scr[...] = carry
    hl_ref[...] = carry

    for cols in blocks:
        local = b_scr[:, cols].reshape(seg_len, _NSEG, lanes)
        prod = a_scr[:, cols].reshape(seg_len, _NSEG, lanes)
        states = (local + prod * hin_scr[:, cols][None]).reshape(ts, lanes)
        if final:
            nat_scr[:, cols] = _from_segment_major(states + hf_ref[:, cols])
        else:
            out_ref[:, cols] = states
    if final:
        out_ref[...] = (nat_scr[...] * gate_ref[...].astype(_F32)).astype(_BF16)


def _rg_scan(u, conv_w, conv_b, w_gates, b_gates, lam, h0, reverse, mode="plain", hf=None, gate=None):
    bsz, s, r = u.shape
    c = _RG_PAIR
    ts = min(s, 1024)
    nchunks = s // ts
    halo = _V7X_SUBLANES
    final = mode == "final"

    def chunk_of(t):
        return (nchunks - 1 - t) if reverse else t

    main = pl.BlockSpec((None, ts, c), lambda b, j, t: (b, chunk_of(t), j))
    prev = pl.BlockSpec((None, halo, c),
                        lambda b, j, t: (b, jnp.maximum(chunk_of(t) * (ts // halo) - 1, 0), j))
    nxt = pl.BlockSpec((None, halo, c),
                       lambda b, j, t: (b, jnp.minimum((chunk_of(t) + 1) * (ts // halo), s // halo - 1), j))
    vec = pl.BlockSpec((1, c), lambda b, j, t: (0, j))
    state = pl.BlockSpec((None, 1, c), lambda b, j, t: (b, 0, j))
    gate_specs = [pl.BlockSpec((None, c, 2 * c), lambda b, j, t: (j, 0, 0)),
                  pl.BlockSpec((None, 1, 2 * c), lambda b, j, t: (j, 0, 0)),
                  vec, state]
    tile = pltpu.VMEM((ts, c), _F32)
    carries = [pltpu.VMEM((1, c), _F32), pltpu.VMEM((_NSEG, c), _F32)]
    out_specs = [main, state]
    out_shape = [jax.ShapeDtypeStruct((bsz, s, r), _BF16 if final else _F32),
                 jax.ShapeDtypeStruct((bsz, 1, r), _F32)]
    if final:
        in_specs = [main] + gate_specs + [main, main]
        args = [u, w_gates, b_gates, lam, h0, hf, gate]
        scratch = [tile, tile] + carries + [tile]
    else:
        in_specs = [main, prev, nxt, pl.BlockSpec((RG_CONV_W, c), lambda b, j, t: (0, j)), vec] + gate_specs
        args = [u, u, u, conv_w, conv_b, w_gates, b_gates, lam, h0]
        conv_in = pltpu.VMEM((ts + (RG_CONV_W - 1) * _NSEG, c), _F32)
        if mode == "first":
            out_specs.append(main)
            out_shape.append(jax.ShapeDtypeStruct((bsz, s, r), _F32))
            scratch = [conv_in, tile, tile] + carries
        else:
            scratch = [conv_in, tile, tile, tile] + carries
    return pl.pallas_call(
        functools.partial(_rg_scan_body, reverse=reverse, mode=mode, ts=ts, nchunks=nchunks),
        grid=(bsz, r // c, nchunks),
        in_specs=in_specs,
        out_specs=out_specs,
        out_shape=out_shape,
        scratch_shapes=scratch,
        compiler_params=_params("parallel", "parallel", "arbitrary"),
        name="rg_scan_%s_%s" % ("bwd" if reverse else "fwd", mode),
    )(*args)


def _mm_res_ln_body(a_ref, w_ref, bias_ref, x_ref, m_ref, g_ref, b_ref, o_ref, *, gate_row, alpha, chunk):
    for r0 in range(0, a_ref.shape[0], chunk):
        rows = slice(r0, r0 + chunk)
        y = _dot(a_ref[rows, :].astype(_BF16), w_ref[...]) + bias_ref[...]
        v = alpha * x_ref[rows, :] + m_ref[gate_row:gate_row + 1, :] * y
        o_ref[rows, :] = _layer_norm(v, g_ref[...], b_ref[...])


def _mm_res_ln(a, w, bias, x, mods, layer, gate_row, ln_g, ln_b, alpha, name):
    bsz, s, d = x.shape
    k = a.shape[-1]
    tm = 512
    return pl.pallas_call(
        functools.partial(_mm_res_ln_body, gate_row=gate_row, alpha=alpha, chunk=256),
        grid=(bsz, s // tm),
        in_specs=[pl.BlockSpec((None, tm, k), lambda b, i: (b, i, 0)),
                  pl.BlockSpec((k, d), lambda b, i: (0, 0)),
                  pl.BlockSpec((1, d), lambda b, i: (0, 0)),
                  pl.BlockSpec((None, tm, d), lambda b, i: (b, i, 0)),
                  _mod_spec(layer, d),
                  _vec_spec(2 * layer, d), _vec_spec(2 * layer, d)],
        out_specs=pl.BlockSpec((None, tm, d), lambda b, i: (b, i, 0)),
        out_shape=jax.ShapeDtypeStruct((bsz, s, d), _F32),
        compiler_params=_params("parallel", "parallel"),
        name=name,
    )(a, w, bias, x, mods, ln_g, ln_b)


def _ffn_body(x_ref, m_ref, w1_ref, w3_ref, w2_ref, g_ref, b_ref, o_ref, h_scr, *, nf, alpha, edge_rows):
    f = pl.program_id(2)
    tm = x_ref.shape[0]

    def step(first, last, chunk):
        for r0 in range(0, tm, chunk):
            rows = slice(r0, r0 + chunk)
            if first:
                h_scr[rows, :] = (x_ref[rows, :] * (1.0 + m_ref[4:5, :]) + m_ref[3:4, :]).astype(_BF16)
            h = h_scr[rows, :]
            g = _dot(h, w1_ref[...])
            u = _dot(h, w3_ref[...])
            act = (g * jax.nn.sigmoid(g) * u).astype(_BF16)
            if first:
                o_ref[rows, :] = _dot(act, w2_ref[...])
            else:
                o_ref[rows, :] += _dot(act, w2_ref[...])
            if last:
                v = alpha * x_ref[rows, :] + m_ref[5:6, :] * o_ref[rows, :]
                o_ref[rows, :] = _layer_norm(v, g_ref[...], b_ref[...])

    @pl.when(f == 0)
    def _():
        step(True, False, edge_rows)

    @pl.when(jnp.logical_and(f > 0, f < nf - 1))
    def _():
        step(False, False, tm)

    @pl.when(f == nf - 1)
    def _():
        step(False, True, edge_rows)


def _ffn(x, mods, layer, w1, w3, w2, ln_g, ln_b, alpha):
    bsz, s, d = x.shape
    dff = w1.shape[2]
    tm, tf = 1024, 512
    nf = dff // tf
    xspec = pl.BlockSpec((None, tm, d), lambda b, i, f: (b, i, 0))
    return pl.pallas_call(
        functools.partial(_ffn_body, nf=nf, alpha=alpha, edge_rows=256),
        grid=(bsz, s // tm, nf),
        in_specs=[xspec,
                  _mod_spec(layer, d),
                  pl.BlockSpec((None, d, tf), lambda b, i, f: (layer, 0, f)),
                  pl.BlockSpec((None, d, tf), lambda b, i, f: (layer, 0, f)),
                  pl.BlockSpec((None, tf, d), lambda b, i, f: (layer, f, 0)),
                  _vec_spec(2 * layer + 1, d), _vec_spec(2 * layer + 1, d)],
        out_specs=xspec,
        out_shape=jax.ShapeDtypeStruct((bsz, s, d), _F32),
        scratch_shapes=[pltpu.VMEM((tm, d), _BF16)],
        compiler_params=_params("parallel", "parallel", "arbitrary", vmem=_VMEM_LIMIT_LARGE),
        name="ffn",
    )(x, mods, w1, w3, w2, ln_g, ln_b)


def _pool_body(x_ref, xp_ref, xn_ref, m_ref, w_ref, pb_ref, ps_ref, g_ref, b_ref, o_ref, hext, y_scr,
               *, tm, nchunks, seq, alpha):
    i = pl.program_id(1)
    halo = _V7X_SUBLANES
    scale = 1.0 + m_ref[1:2, :]
    shift = m_ref[0:1, :]
    hext[0:halo, :] = jnp.where(i == 0, 0.0, xp_ref[...] * scale + shift)
    hext[halo:halo + tm, :] = x_ref[...] * scale + shift
    hext[halo + tm:2 * halo + tm, :] = jnp.where(i == nchunks - 1, 0.0, xn_ref[...] * scale + shift)
    t = i * tm + lax.broadcasted_iota(jnp.int32, (tm, 1), 0)
    gd = x_ref.shape[1] // len(POOL_WINDOWS)
    for g, win in enumerate(POOL_WINDOWS):
        cols = slice(g * gd, (g + 1) * gd)
        lo = jnp.maximum(t - win // 2, 0)
        hi = jnp.minimum(t - win // 2 + win, seq)
        inv = 1.0 / (hi - lo).astype(_F32)
        acc = hext[pl.ds(halo - win // 2, tm), cols]
        for j in range(1, win):
            acc = acc + hext[pl.ds(halo - win // 2 + j, tm), cols]
        p = acc * inv - hext[halo:halo + tm, cols]
        y = _dot(p.astype(_BF16), w_ref[g]) + pb_ref[:, cols]
        y_scr[:, cols] = y * ps_ref[:, cols]
    v = alpha * x_ref[...] + m_ref[2:3, :] * y_scr[...]
    o_ref[...] = _layer_norm(v, g_ref[...], b_ref[...])


def _pool_layer(x, mods, layer, pool_w, pool_b, pool_scale, ln_g, ln_b, alpha):
    bsz, s, d = x.shape
    tm = 512
    nchunks = s // tm
    halo = _V7X_SUBLANES
    row = pl.BlockSpec((1, d), lambda b, i: (0, 0))
    xspec = pl.BlockSpec((None, tm, d), lambda b, i: (b, i, 0))
    return pl.pallas_call(
        functools.partial(_pool_body, tm=tm, nchunks=nchunks, seq=s, alpha=alpha),
        grid=(bsz, nchunks),
        in_specs=[xspec,
                  pl.BlockSpec((None, halo, d), lambda b, i: (b, jnp.maximum(i * (tm // halo) - 1, 0), 0)),
                  pl.BlockSpec((None, halo, d),
                               lambda b, i: (b, jnp.minimum((i + 1) * (tm // halo), s // halo - 1), 0)),
                  _mod_spec(layer, d),
                  pl.BlockSpec(pool_w.shape, lambda b, i: (0, 0, 0)),
                  row, row,
                  _vec_spec(2 * layer, d), _vec_spec(2 * layer, d)],
        out_specs=xspec,
        out_shape=jax.ShapeDtypeStruct((bsz, s, d), _F32),
        scratch_shapes=[pltpu.VMEM((tm + 2 * halo, d), _F32), pltpu.VMEM((tm, d), _F32)],
        compiler_params=_params("parallel", "parallel"),
        name="pool_mixer",
    )(x, x, x, mods, pool_w, pool_b, pool_scale, ln_g, ln_b)


def _conf_in_body(x_ref, m_ref, wa_ref, wb_ref, ba_ref, bb_ref, o_ref, h_scr, *, first_chunk):
    tm = x_ref.shape[0]

    def step(first, chunk):
        for r0 in range(0, tm, chunk):
            rows = slice(r0, r0 + chunk)
            if first:
                h_scr[rows, :] = (x_ref[rows, :] * (1.0 + m_ref[1:2, :]) + m_ref[0:1, :]).astype(_BF16)
            h = h_scr[rows, :]
            ua = _dot(h, wa_ref[...]) + ba_ref[...]
            ub = _dot(h, wb_ref[...]) + bb_ref[...]
            o_ref[rows, :] = ua * jax.nn.sigmoid(ub)

    @pl.when(pl.program_id(2) == 0)
    def _():
        step(True, first_chunk)

    @pl.when(pl.program_id(2) > 0)
    def _():
        step(False, tm)


def _conf_in(x, mods, layer, w1, b1):
    bsz, s, d = x.shape
    tm, tn = 1024, 1024
    nj = d // tn
    return pl.pallas_call(
        functools.partial(_conf_in_body, first_chunk=256),
        grid=(bsz, s // tm, nj),
        in_specs=[pl.BlockSpec((None, tm, d), lambda b, i, j: (b, i, 0)),
                  _mod_spec(layer, d),
                  pl.BlockSpec((d, tn), lambda b, i, j: (0, j)),
                  pl.BlockSpec((d, tn), lambda b, i, j: (0, j + nj)),
                  pl.BlockSpec((1, tn), lambda b, i, j: (0, j)),
                  pl.BlockSpec((1, tn), lambda b, i, j: (0, j + nj))],
        out_specs=pl.BlockSpec((None, tm, tn), lambda b, i, j: (b, i, j)),
        out_shape=jax.ShapeDtypeStruct((bsz, s, d), _F32),
        scratch_shapes=[pltpu.VMEM((tm, d), _BF16)],
        compiler_params=_params("parallel", "parallel", "arbitrary", vmem=_VMEM_LIMIT_LARGE),
        name="conf_in",
    )(x, mods, w1, w1, b1, b1)


_CONF_HALO = 16


def _conf_out_body(u_ref, up_ref, un_ref, x_ref, m_ref, dw_ref, cg_ref, cb_ref, w2_ref, b2_ref,
                   g_ref, b_ref, o_ref, uslab, pslab, nslab, uperm, conv_scr, *, tm, nchunks, alpha):
    i = pl.program_id(1)
    halo = _CONF_HALO
    lanes = _V7X_LANES
    d = x_ref.shape[1]
    ncol = d // lanes
    seg_len = tm // _NSEG
    pad = (CONF_KERNEL - 1) // 2
    first = i == 0
    last = i == nchunks - 1
    for cb_ in range(ncol):
        cols = slice(cb_ * lanes, (cb_ + 1) * lanes)
        uslab[cb_] = u_ref[:, cols]
        pslab[cb_] = jnp.where(first, 0.0, up_ref[:, cols])
        nslab[cb_] = jnp.where(last, 0.0, un_ref[:, cols])

    def lane_block(cb_, carry):
        perm = _to_segment_major(uslab[cb_])
        uperm[pad * _NSEG:pad * _NSEG + tm, :] = perm
        for t in range(-pad, 0):
            src_rows = slice((seg_len + t) * _NSEG, (seg_len + t + 1) * _NSEG)
            uperm[(t + pad) * _NSEG:(t + pad + 1) * _NSEG, :] = _neighbour_segment_tile(
                perm[src_rows, :], pslab[cb_, halo + t:halo + t + 1, :], toward_next=False)
        for t in range(seg_len, seg_len + pad):
            src_rows = slice((t - seg_len) * _NSEG, (t - seg_len + 1) * _NSEG)
            uperm[(t + pad) * _NSEG:(t + pad + 1) * _NSEG, :] = _neighbour_segment_tile(
                perm[src_rows, :], nslab[cb_, t - seg_len:t - seg_len + 1, :], toward_next=True)
        w = [jnp.broadcast_to(dw_ref[cb_, k:k + 1, :], (_NSEG, lanes)) for k in range(CONF_KERNEL)]
        bias = jnp.broadcast_to(dw_ref[cb_, CONF_KERNEL:CONF_KERNEL + 1, :], (_NSEG, lanes))
        for j in range(0, seg_len, 2):
            acc0 = bias
            acc1 = bias
            for q in range(j, j + CONF_KERNEL + 1):
                tap = uperm[q * _NSEG:(q + 1) * _NSEG, :]
                if q - j < CONF_KERNEL:
                    acc0 = acc0 + w[q - j] * tap
                if q - j >= 1:
                    acc1 = acc1 + w[q - j - 1] * tap
            conv_scr[cb_, j * _NSEG:(j + 1) * _NSEG, :] = acc0
            conv_scr[cb_, (j + 1) * _NSEG:(j + 2) * _NSEG, :] = acc1
        return carry

    lax.fori_loop(0, ncol, lane_block, 0)

    cv = jnp.concatenate([conv_scr[cb_] for cb_ in range(ncol)], axis=-1)
    u2 = _layer_norm(cv, cg_ref[...], cb_ref[...])
    u2 = u2 * jax.nn.sigmoid(u2)
    y = _from_segment_major(_dot(u2.astype(_BF16), w2_ref[...]) + b2_ref[...])
    v = alpha * x_ref[...] + m_ref[2:3, :] * y
    o_ref[...] = _layer_norm(v, g_ref[...], b_ref[...])


def _conf_out(u, x, mods, layer, dw_blocks, cg, cb, w2, b2, ln_g, ln_b, alpha):
    bsz, s, d = x.shape
    tm = 256
    nchunks = s // tm
    halo = _CONF_HALO
    ncol = d // _V7X_LANES
    row = pl.BlockSpec((1, d), lambda b, i: (0, 0))
    xspec = pl.BlockSpec((None, tm, d), lambda b, i: (b, i, 0))
    return pl.pallas_call(
        functools.partial(_conf_out_body, tm=tm, nchunks=nchunks, alpha=alpha),
        grid=(bsz, nchunks),
        in_specs=[xspec,
                  pl.BlockSpec((None, halo, d), lambda b, i: (b, jnp.maximum(i * (tm // halo) - 1, 0), 0)),
                  pl.BlockSpec((None, halo, d),
                               lambda b, i: (b, jnp.minimum((i + 1) * (tm // halo), s // halo - 1), 0)),
                  xspec,
                  _mod_spec(layer, d),
                  pl.BlockSpec(dw_blocks.shape, lambda b, i: (0, 0, 0)),
                  row, row,
                  pl.BlockSpec((d, d), lambda b, i: (0, 0)),
                  row,
                  _vec_spec(2 * layer, d), _vec_spec(2 * layer, d)],
        out_specs=xspec,
        out_shape=jax.ShapeDtypeStruct((bsz, s, d), _F32),
        scratch_shapes=[pltpu.VMEM((ncol, tm, _V7X_LANES), _F32),
                        pltpu.VMEM((ncol, halo, _V7X_LANES), _F32),
                        pltpu.VMEM((ncol, halo, _V7X_LANES), _F32),
                        pltpu.VMEM((tm + (CONF_KERNEL - 1) * _NSEG, _V7X_LANES), _F32),
                        pltpu.VMEM((ncol, tm, _V7X_LANES), _F32)],
        compiler_params=_params("parallel", "parallel"),
        name="conf_out",
    )(u, u, u, x, mods, dw_blocks, cg, cb, w2, b2, ln_g, ln_b)


def _ft_in_body(x_ref, m_ref, w_ref, zr_ref, zi_ref, *, groups):
    h = (x_ref[...] * (1.0 + m_ref[1:2, :]) + m_ref[0:1, :]).astype(_BF16)
    gd = x_ref.shape[1] // groups
    for g in range(groups):
        cols = slice(g * gd, (g + 1) * gd)
        z = _dot(h[:, cols], w_ref[...])
        zr_ref[:, cols] = z[:, :gd].astype(_BF16)
        zi_ref[:, cols] = z[:, gd:].astype(_BF16)


def _ft_in(x, mods, layer, w_dft):
    bsz, s, d = x.shape
    tm = 512
    xspec = pl.BlockSpec((None, tm, d), lambda b, i: (b, i, 0))
    return pl.pallas_call(
        functools.partial(_ft_in_body, groups=FT_GROUPS),
        grid=(bsz, s // tm),
        in_specs=[xspec, _mod_spec(layer, d),
                  pl.BlockSpec(w_dft.shape, lambda b, i: (0, 0))],
        out_specs=[xspec, xspec],
        out_shape=[jax.ShapeDtypeStruct((bsz, s, d), _BF16)] * 2,
        compiler_params=_params("parallel", "parallel"),
        name="ft_channel_dft",
    )(x, mods, w_dft)


def _swap_major(v, n_major):
    rows, c = v.shape
    n_minor = rows // n_major
    return pltpu.einshape("abc->bac", v.reshape(n_major, n_minor, c)).reshape(rows, c)


def _ft_seq_body(zr_ref, zi_ref, ma_ref, w2_ref, o_ref, pr_scr, pi_scr, yr_scr, yi_scr, f_scr, *, n1, n2):
    pr_scr[...] = _swap_major(zr_ref[...], n1)
    pi_scr[...] = _swap_major(zi_ref[...], n1)
    for s2 in range(n2):
        rows = slice(s2 * n1, (s2 + 1) * n1)
        zc = jnp.concatenate([pr_scr[rows, :], pi_scr[rows, :]], axis=0)
        y = _dot(ma_ref[s2], zc).astype(_BF16)
        yr_scr[rows, :] = y[:n1]
        yi_scr[rows, :] = y[n1:]
    pr_scr[...] = _swap_major(yr_scr[...], n2)
    pi_scr[...] = _swap_major(yi_scr[...], n2)
    for k1 in range(n1):
        rows = slice(k1 * n2, (k1 + 1) * n2)
        yc = jnp.concatenate([pr_scr[rows, :], pi_scr[rows, :]], axis=0)
        f_scr[rows, :] = _dot(w2_ref[...], yc)
    o_ref[...] = _swap_major(f_scr[...], n1)


def _ft_seq(zr, zi, ma, w2cat, n1, n2):
    bsz, s, d = zr.shape
    tc = _V7X_LANES
    spec = pl.BlockSpec((None, s, tc), lambda b, j: (b, 0, j))
    return pl.pallas_call(
        functools.partial(_ft_seq_body, n1=n1, n2=n2),
        grid=(bsz, d // tc),
        in_specs=[spec, spec,
                  pl.BlockSpec(ma.shape, lambda b, j: (0, 0, 0)),
                  pl.BlockSpec(w2cat.shape, lambda b, j: (0, 0))],
        out_specs=spec,
        out_shape=jax.ShapeDtypeStruct((bsz, s, d), _F32),
        scratch_shapes=[pltpu.VMEM((s, tc), _BF16)] * 4 + [pltpu.VMEM((s, tc), _F32)],
        compiler_params=_params("parallel", "parallel"),
        name="ft_sequence_dft",
    )(zr, zi, ma, w2cat)


def _ft_tables(s, gd):
    n2 = 128
    n1 = s // n2
    p = np.arange(gd)
    ang = 2.0 * np.pi * ((p[:, None] * p[None, :]) % gd) / gd
    scale = 1.0 / np.sqrt(float(s) * gd)
    w_dft = np.concatenate([np.cos(ang), -np.sin(ang)], axis=1) * scale
    k1 = np.arange(n1)
    s1 = np.arange(n1)
    s2 = np.arange(n2)
    pos = n2 * s1[None, None, :] + s2[:, None, None]
    ang = 2.0 * np.pi * ((k1[None, :, None] * pos) % s) / s
    er, ei = np.cos(ang), np.sin(ang)
    ma = np.concatenate([np.concatenate([er, ei], axis=2),
                         np.concatenate([-ei, er], axis=2)], axis=1)
    k2 = np.arange(n2)
    ang = 2.0 * np.pi * ((k2[:, None] * s2[None, :]) % n2) / n2
    w2cat = np.concatenate([np.cos(ang), np.sin(ang)], axis=1)
    as_bf16 = lambda a: jnp.asarray(a.astype(np.float32)).astype(_BF16)
    return as_bf16(w_dft), as_bf16(ma), as_bf16(w2cat), n1, n2


def _pad_blocks_last(a):
    lead = a.shape[:-1]
    blk = a.shape[-1] // RG_BLOCKS
    a = a.reshape(lead + (RG_BLOCKS, blk))
    a = jnp.pad(a, [(0, 0)] * len(lead) + [(0, 0), (0, _RG_BLOCK_PAD - blk)])
    return a.reshape(lead + (RG_BLOCKS * _RG_BLOCK_PAD,))


def _pair_block_diag(w):
    blk = w.shape[-1]
    w = jnp.pad(w, [(0, 0), (0, _RG_BLOCK_PAD - blk), (0, _RG_BLOCK_PAD - blk)])
    w = w.reshape(RG_BLOCKS // 2, 2, _RG_BLOCK_PAD, _RG_BLOCK_PAD)
    zero = jnp.zeros_like(w[:, 0])
    top = jnp.concatenate([w[:, 0], zero], axis=-1)
    bottom = jnp.concatenate([zero, w[:, 1]], axis=-1)
    return jnp.concatenate([top, bottom], axis=1)


def _pos_tables(rows, cols, dim):
    quarter = dim // 4
    omega = 1.0 / (POS_THETA ** (jnp.arange(quarter, dtype=_F32) / quarter))
    ar = jnp.arange(rows, dtype=_F32)[:, None] * omega[None]
    ac = jnp.arange(cols, dtype=_F32)[:, None] * omega[None]
    er = jnp.concatenate([jnp.sin(ar), jnp.cos(ar)], axis=-1)
    ec = jnp.concatenate([jnp.sin(ac), jnp.cos(ac)], axis=-1)
    return er, ec


def _rglru_layer(x, ctx, mods, layer, bsz, w_gate, w_x, conv_w, conv_b, wr, br, wi, bi, lam, w_out,
                 ln_g, ln_b, alpha):
    s, d = x.shape[1], x.shape[2]
    w_x_p = _pad_blocks_last(w_x).astype(_BF16)
    w_gate_p = _pad_blocks_last(w_gate).astype(_BF16)
    conv_w_p = _pad_blocks_last(conv_w)
    conv_b_p = _pad_blocks_last(conv_b)[None]
    blk = w_out.shape[0] // RG_BLOCKS
    w_out_p = jnp.pad(w_out.reshape(RG_BLOCKS, blk, d), [(0, 0), (0, _RG_BLOCK_PAD - blk), (0, 0)])
    w_out_p = w_out_p.reshape(RG_BLOCKS * _RG_BLOCK_PAD, d).astype(_BF16)
    npairs = RG_BLOCKS // 2
    er_ec = _pos_tables(s // GRID_W, GRID_W, d)
    x0, xw, gate = _rg_in(x, mods, layer, lambda b: b, w_x_p, w_gate_p, er_ec)
    xwc = _rg_in(ctx, mods, layer, lambda b: bsz, w_x_p, None, None)
    zeros_state = jnp.zeros((bsz, 1, xw.shape[-1]), _F32)
    hf = None
    out = None
    for direction in range(2):
        reverse = direction == 1
        w_gates = (0.5 * jnp.concatenate([_pair_block_diag(wr[direction]), _pair_block_diag(wi[direction])],
                                         axis=-1)).astype(_BF16)
        b_gates = 0.5 * jnp.concatenate([_pad_blocks_last(br[direction]).reshape(npairs, 1, _RG_PAIR),
                                         _pad_blocks_last(bi[direction]).reshape(npairs, 1, _RG_PAIR)], axis=-1)
        lam_p = _pad_blocks_last(lam[direction])[None]
        _, h0 = _rg_scan(xwc, conv_w_p, conv_b_p, w_gates, b_gates, lam_p, zeros_state, reverse)
        if not reverse:
            hf, _, xb = _rg_scan(xw, conv_w_p, conv_b_p, w_gates, b_gates, lam_p, h0, reverse, "first")
        else:
            out, _ = _rg_scan(xb, None, None, w_gates, b_gates, lam_p, h0, reverse, "final", hf, gate)
    zero_bias = jnp.zeros((1, d), _F32)
    return _mm_res_ln(out, w_out_p, zero_bias, x0, mods, layer, 2, ln_g, ln_b, alpha, "rg_out")


def kernel(x, c, ctx, c_ctx, mod_w, mod_b, ln_g, ln_b, ffn_w1, ffn_w3, ffn_w2, rg_w_gate, rg_w_x, rg_conv_w, rg_conv_b, rg_wr, rg_br, rg_wi, rg_bi, rg_lam, rg_w_out, pool_w, pool_b, pool_scale, cv_w1, cv_b1, cv_dw, cv_dwb, cv_ln_g, cv_ln_b, cv_w2, cv_b2, ft_w, ft_b):
    bsz, s, d = x.shape
    depth = mod_w.shape[0]
    alpha = float((2 * depth) ** 0.25)
    assert depth == N_MIXERS and bsz < _MOD_ROWS, "one layer per mixer kind; the context feeds layer 0 only"

    cond = jnp.zeros((_MOD_ROWS, d), _F32).at[:bsz].set(c).at[bsz].set(c_ctx)
    mods = _cond_vectors(cond, mod_w, mod_b).reshape(depth, _MOD_ROWS, 6, d)
    ln_g3 = ln_g.reshape(depth * 2, 1, d)
    ln_b3 = ln_b.reshape(depth * 2, 1, d)
    w1_all = ffn_w1.astype(_BF16)
    w3_all = ffn_w3.astype(_BF16)
    w2_all = ffn_w2.astype(_BF16)

    def row(v):
        return v.reshape(1, -1)

    def ffn(xc, layer):
        return _ffn(xc, mods, layer, w1_all, w3_all, w2_all, ln_g3, ln_b3, alpha)

    xc = _rglru_layer(x, ctx, mods, 0, bsz, rg_w_gate[0], rg_w_x[0], rg_conv_w[0], rg_conv_b[0],
                      rg_wr[0], rg_br[0], rg_wi[0], rg_bi[0], rg_lam[0], rg_w_out[0], ln_g3, ln_b3, alpha)
    xc = ffn(xc, 0)

    xc = _pool_layer(xc, mods, 1, pool_w[0].astype(_BF16), row(pool_b[0]), row(pool_scale[0]),
                     ln_g3, ln_b3, alpha)
    xc = ffn(xc, 1)

    u = _conf_in(xc, mods, 2, cv_w1[0].astype(_BF16), row(cv_b1[0]))
    ncol = d // _V7X_LANES
    dw_blocks = jnp.concatenate([cv_dw[0], cv_dwb[0][None]], axis=0)
    dw_blocks = dw_blocks.reshape(CONF_KERNEL + 1, ncol, _V7X_LANES).transpose(1, 0, 2)
    xc = _conf_out(u, xc, mods, 2, dw_blocks, row(cv_ln_g[0]), row(cv_ln_b[0]),
                   cv_w2[0].astype(_BF16), row(cv_b2[0]), ln_g3, ln_b3, alpha)
    xc = ffn(xc, 2)

    w_dft, ma, w2cat, n1, n2 = _ft_tables(s, d // FT_GROUPS)
    zr, zi = _ft_in(xc, mods, 3, w_dft)
    f = _ft_seq(zr, zi, ma, w2cat, n1, n2)
    xc = _mm_res_ln(f, ft_w[0].astype(_BF16), row(ft_b[0]), xc, mods, 3, 2, ln_g3, ln_b3, alpha, "ft_out")
    xc = ffn(xc, 3)
    return xc
```

```python
import functools

import numpy as np
import jax
import jax.numpy as jnp
from jax import lax
from jax.experimental import pallas as pl
from jax.experimental.pallas import tpu as pltpu

_F32 = jnp.float32
_BF16 = jnp.bfloat16

GRID_W = 64
N_MIXERS = 4
RG_BLOCKS = 16
RG_CONV_W = 4
RG_C = 8.0
POOL_WINDOWS = (2, 4, 8, 16)
CONF_KERNEL = 31
FT_GROUPS = 4
LN_EPS = 1e-5
POS_THETA = 10000.0

_V7X_LANES = 128
_V7X_SUBLANES = 8
_V7X_VMEM_BYTES = 64 * 1024 * 1024
_VMEM_LIMIT = _V7X_VMEM_BYTES - 12 * 1024 * 1024
_VMEM_LIMIT_LARGE = _V7X_VMEM_BYTES - 4 * 1024 * 1024

_RG_BLOCK_PAD = 192
_RG_PAIR = 2 * _RG_BLOCK_PAD
_RG_COEF_ROWS = 256

_MOD_ROWS = 8
_NSEG = _V7X_SUBLANES
_SQRT_FLOOR = 1e-30


def _params(*sem, vmem=_VMEM_LIMIT):
    return pltpu.CompilerParams(dimension_semantics=sem, vmem_limit_bytes=vmem)


def _layer_norm(v, g, b):
    mu = jnp.mean(v, axis=-1, keepdims=True)
    xc = v - mu
    var = jnp.mean(xc * xc, axis=-1, keepdims=True)
    return xc * lax.rsqrt(var + LN_EPS) * g + b


def _dot(a, b):
    return jnp.dot(a, b, preferred_element_type=_F32)


def _to_segment_major(v):
    rows, c = v.shape
    return pltpu.einshape("mjc->jmc", v.reshape(_NSEG, rows // _NSEG, c)).reshape(rows, c)


def _from_segment_major(v):
    rows, c = v.shape
    return pltpu.einshape("jmc->mjc", v.reshape(rows // _NSEG, _NSEG, c)).reshape(rows, c)


def _neighbour_segment_tile(tile, edge_row, toward_next):
    sub = lax.broadcasted_iota(jnp.int32, tile.shape, 0)
    if toward_next:
        return jnp.where(sub == _NSEG - 1, edge_row, pltpu.roll(tile, shift=_NSEG - 1, axis=0))
    return jnp.where(sub == 0, edge_row, pltpu.roll(tile, shift=1, axis=0))


def _mod_spec(layer, d, row_of_batch=lambda b: b):
    return pl.BlockSpec((None, None, 6, d), lambda b, *_: (layer, row_of_batch(b), 0, 0))


def _vec_spec(index, d):
    return pl.BlockSpec((None, 1, d), lambda *_: (index, 0, 0))


def _cond_body(c_ref, w_ref, b_ref, o_ref):
    cond = c_ref[...]
    cond = cond * jax.nn.sigmoid(cond)
    o_ref[...] = _dot(cond.astype(_BF16), w_ref[...].astype(_BF16)) + b_ref[...]


def _cond_vectors(cond, mod_w, mod_b):
    depth, d, n = mod_w.shape
    tn = 1024
    return pl.pallas_call(
        _cond_body,
        grid=(depth, n // tn),
        in_specs=[pl.BlockSpec((_MOD_ROWS, d), lambda l, j: (0, 0)),
                  pl.BlockSpec((None, d, tn), lambda l, j: (l, 0, j)),
                  pl.BlockSpec((None, 1, tn), lambda l, j: (l, 0, j))],
        out_specs=pl.BlockSpec((None, _MOD_ROWS, tn), lambda l, j: (l, 0, j)),
        out_shape=jax.ShapeDtypeStruct((depth, _MOD_ROWS, n), _F32),
        compiler_params=_params("parallel", "parallel"),
        name="cond_vectors",
    )(cond, mod_w, mod_b.reshape(depth, 1, n))


def _rg_in_body(*refs, latent, tm, first_chunk):
    if latent:
        x_ref, er_ref, ec_ref, m_ref, wx_ref, wg_ref, x0_ref, xw_ref, gate_ref, h_scr = refs
    else:
        x_ref, m_ref, wx_ref, xw_ref, h_scr = refs

    def step(first, chunk):
        for r0 in range(0, tm, chunk):
            rows = slice(r0, r0 + chunk)
            if first:
                if latent:
                    half = er_ref.shape[1]
                    for r in range(r0 // GRID_W, (r0 + chunk) // GRID_W):
                        grid_row = slice(r * GRID_W, (r + 1) * GRID_W)
                        x0_ref[grid_row, :half] = x_ref[grid_row, :half] + er_ref[r:r + 1, :]
                        x0_ref[grid_row, half:] = x_ref[grid_row, half:] + ec_ref[...]
                    x0 = x0_ref[rows, :]
                else:
                    x0 = x_ref[rows, :]
                h_scr[rows, :] = (x0 * (1.0 + m_ref[1:2, :]) + m_ref[0:1, :]).astype(_BF16)
            h = h_scr[rows, :]
            xw_ref[rows, :] = _dot(h, wx_ref[...])
            if latent:
                gate_ref[rows, :] = jax.nn.gelu(_dot(h, wg_ref[...])).astype(_BF16)

    @pl.when(pl.program_id(2) == 0)
    def _():
        step(True, first_chunk)

    @pl.when(pl.program_id(2) > 0)
    def _():
        step(False, tm)


def _rg_in(x, mods, layer, mod_row, w_x, w_gate, pos_tables):
    bsz, s, d = x.shape
    r = w_x.shape[1]
    latent = pos_tables is not None
    tm = 1024 if latent else s
    tn = 512
    grid = (bsz, s // tm, r // tn)
    xspec = pl.BlockSpec((None, tm, d), lambda b, i, j: (b, i, 0))
    mspec = _mod_spec(layer, d, mod_row)
    wspec = pl.BlockSpec((d, tn), lambda b, i, j: (0, j))
    ospec = pl.BlockSpec((None, tm, tn), lambda b, i, j: (b, i, j))
    body = functools.partial(_rg_in_body, latent=latent, tm=tm, first_chunk=min(tm, 256))
    scratch = [pltpu.VMEM((tm, d), _BF16)]
    if latent:
        er, ec = pos_tables
        rows_per_tile = tm // GRID_W
        return pl.pallas_call(
            body, grid=grid,
            in_specs=[xspec,
                      pl.BlockSpec((rows_per_tile, er.shape[1]), lambda b, i, j: (i, 0)),
                      pl.BlockSpec(ec.shape, lambda b, i, j: (0, 0)),
                      mspec, wspec, wspec],
            out_specs=[xspec, ospec, ospec],
            out_shape=[jax.ShapeDtypeStruct((bsz, s, d), _F32),
                       jax.ShapeDtypeStruct((bsz, s, r), _F32),
                       jax.ShapeDtypeStruct((bsz, s, r), _BF16)],
            scratch_shapes=scratch,
            compiler_params=_params("parallel", "parallel", "arbitrary", vmem=_VMEM_LIMIT_LARGE),
            name="rg_in_latent",
        )(x, er, ec, mods, w_x, w_gate)
    return pl.pallas_call(
        body, grid=grid,
        in_specs=[xspec, mspec, wspec],
        out_specs=ospec,
        out_shape=jax.ShapeDtypeStruct((bsz, s, r), _F32),
        scratch_shapes=scratch,
        compiler_params=_params("parallel", "parallel", "arbitrary"),
        name="rg_in_context",
    )(x, mods, w_x)


def _rg_scan_body(*refs, reverse, mode, ts, nchunks):
    final = mode == "final"
    if final:
        (xb_ref, wg_ref, bg_ref, lam_ref, h0_ref, hf_ref, gate_ref,
         out_ref, hl_ref, a_scr, b_scr, h_scr, hin_scr, nat_scr) = refs
    elif mode == "first":
        (u_ref, up_ref, un_ref, cw_ref, cb_ref, wg_ref, bg_ref, lam_ref, h0_ref,
         out_ref, hl_ref, xb_ref, uperm, a_scr, b_scr, h_scr, hin_scr) = refs
    else:
        (u_ref, up_ref, un_ref, cw_ref, cb_ref, wg_ref, bg_ref, lam_ref, h0_ref,
         out_ref, hl_ref, uperm, xb_ref, a_scr, b_scr, h_scr, hin_scr) = refs
    step = pl.program_id(2)
    chunk = (nchunks - 1 - step) if reverse else step
    halo = _V7X_SUBLANES
    seg_len = ts // _NSEG
    c = xb_ref.shape[1]

    @pl.when(step == 0)
    def _():
        h_scr[...] = h0_ref[...]

    lanes = _V7X_LANES
    ncol = c // lanes
    blocks = [slice(i * lanes, (i + 1) * lanes) for i in range(ncol)]

    pad_lo = (RG_CONV_W - 1) // 2
    pad_hi = RG_CONV_W - 1 - pad_lo
    for cols in ([] if final else blocks):
        perm = _to_segment_major(u_ref[:, cols])
        uperm[pad_lo * _NSEG:pad_lo * _NSEG + ts, cols] = perm
        for i in range(-pad_lo, 0):
            edge = jnp.where(chunk == 0, 0.0, up_ref[halo + i:halo + i + 1, cols])
            src_rows = slice((seg_len + i) * _NSEG, (seg_len + i + 1) * _NSEG)
            uperm[(i + pad_lo) * _NSEG:(i + pad_lo + 1) * _NSEG, cols] = _neighbour_segment_tile(
                perm[src_rows, :], edge, toward_next=False)
        for i in range(seg_len, seg_len + pad_hi):
            edge = jnp.where(chunk == nchunks - 1, 0.0, un_ref[i - seg_len:i - seg_len + 1, cols])
            src_rows = slice((i - seg_len) * _NSEG, (i - seg_len + 1) * _NSEG)
            uperm[(i + pad_lo) * _NSEG:(i + pad_lo + 1) * _NSEG, cols] = _neighbour_segment_tile(
                perm[src_rows, :], edge, toward_next=True)
        cw = [jnp.broadcast_to(cw_ref[k:k + 1, cols], (_NSEG, lanes)) for k in range(RG_CONV_W)]
        cb = jnp.broadcast_to(cb_ref[:, cols], (_NSEG, lanes))
        taps = [uperm[q * _NSEG:(q + 1) * _NSEG, cols] for q in range(seg_len + RG_CONV_W - 1)]
        for j in range(seg_len):
            acc = cb
            for k in range(RG_CONV_W):
                acc = acc + cw[k] * taps[j + k]
            xb_ref[j * _NSEG:(j + 1) * _NSEG, cols] = acc

    lam = lam_ref[...]
    softplus_neg_lam = jnp.maximum(-lam, 0.0) + jnp.log1p(jnp.exp(-jnp.abs(lam)))
    c0 = (-0.5 * RG_C) * softplus_neg_lam
    coef_rows = min(ts, _RG_COEF_ROWS)
    for r0 in range(0, ts, coef_rows):
        rows = slice(r0, r0 + coef_rows)
        xb = xb_ref[rows, :]
        g = _dot(xb.astype(_BF16), wg_ref[...]) + bg_ref[...]
        t_r = jnp.tanh(g[:, :c])
        t_i = jnp.tanh(g[:, c:])
        log_a = c0 * t_r + c0
        a = jnp.exp(log_a)
        z = jnp.tanh(-log_a) * (1.0 + a * a)
        root = z * lax.rsqrt(jnp.maximum(z, _SQRT_FLOOR))
        a_scr[rows, :] = a
        b_scr[rows, :] = root * ((0.5 * t_i + 0.5) * xb)

    h = [jnp.zeros((_NSEG, lanes), _F32)] * ncol
    p = [jnp.ones((_NSEG, lanes), _F32)] * ncol
    order = range(seg_len - 1, -1, -1) if reverse else range(seg_len)
    for j in order:
        rows = slice(j * _NSEG, (j + 1) * _NSEG)
        for i, cols in enumerate(blocks):
            aj = a_scr[rows, cols]
            h[i] = aj * h[i] + b_scr[rows, cols]
            p[i] = aj * p[i]
            b_scr[rows, cols] = h[i]
            a_scr[rows, cols] = p[i]
    j_last = 0 if reverse else seg_len - 1

    carry = h_scr[...]
    for m in (range(_NSEG - 1, -1, -1) if reverse else range(_NSEG)):
        hin_scr[m:m + 1, :] = carry
        row = j_last * _NSEG + m
        carry = b_scr[row:row + 1, :] + a_scr[row:row + 1, :] * carry
    h_scr[...] = carry
    hl_ref[...] = carry

    for cols in blocks:
        local = b_scr[:, cols].reshape(seg_len, _NSEG, lanes)
        prod = a_scr[:, cols].reshape(seg_len, _NSEG, lanes)
        states = (local + prod * hin_scr[:, cols][None]).reshape(ts, lanes)
        if final:
            nat_scr[:, cols] = _from_segment_major(states + hf_ref[:, cols])
        else:
            out_ref[:, cols] = states
    if final:
        out_ref[...] = (nat_scr[...] * gate_ref[...].astype(_F32)).astype(_BF16)


def _rg_scan(u, conv_w, conv_b, w_gates, b_gates, lam, h0, reverse, mode="plain", hf=None, gate=None):
    bsz, s, r = u.shape
    c = _RG_PAIR
    ts = min(s, 1024)
    nchunks = s // ts
    halo = _V7X_SUBLANES
    final = mode == "final"

    def chunk_of(t):
        return (nchunks - 1 - t) if reverse else t

    main = pl.BlockSpec((None, ts, c), lambda b, j, t: (b, chunk_of(t), j))
    prev = pl.BlockSpec((None, halo, c),
                        lambda b, j, t: (b, jnp.maximum(chunk_of(t) * (ts // halo) - 1, 0), j))
    nxt = pl.BlockSpec((None, halo, c),
                       lambda b, j, t: (b, jnp.minimum((chunk_of(t) + 1) * (ts // halo), s // halo - 1), j))
    vec = pl.BlockSpec((1, c), lambda b, j, t: (0, j))
    state = pl.BlockSpec((None, 1, c), lambda b, j, t: (b, 0, j))
    gate_specs = [pl.BlockSpec((None, c, 2 * c), lambda b, j, t: (j, 0, 0)),
                  pl.BlockSpec((None, 1, 2 * c), lambda b, j, t: (j, 0, 0)),
                  vec, state]
    tile = pltpu.VMEM((ts, c), _F32)
    carries = [pltpu.VMEM((1, c), _F32), pltpu.VMEM((_NSEG, c), _F32)]
    out_specs = [main, state]
    out_shape = [jax.ShapeDtypeStruct((bsz, s, r), _BF16 if final else _F32),
                 jax.ShapeDtypeStruct((bsz, 1, r), _F32)]
    if final:
        in_specs = [main] + gate_specs + [main, main]
        args = [u, w_gates, b_gates, lam, h0, hf, gate]
        scratch = [tile, tile] + carries + [tile]
    else:
        in_specs = [main, prev, nxt, pl.BlockSpec((RG_CONV_W, c), lambda b, j, t: (0, j)), vec] + gate_specs
        args = [u, u, u, conv_w, conv_b, w_gates, b_gates, lam, h0]
        conv_in = pltpu.VMEM((ts + (RG_CONV_W - 1) * _NSEG, c), _F32)
        if mode == "first":
            out_specs.append(main)
            out_shape.append(jax.ShapeDtypeStruct((bsz, s, r), _F32))
            scratch = [conv_in, tile, tile] + carries
        else:
            scratch = [conv_in, tile, tile, tile] + carries
    return pl.pallas_call(
        functools.partial(_rg_scan_body, reverse=reverse, mode=mode, ts=ts, nchunks=nchunks),
        grid=(bsz, r // c, nchunks),
        in_specs=in_specs,
        out_specs=out_specs,
        out_shape=out_shape,
        scratch_shapes=scratch,
        compiler_params=_params("parallel", "parallel", "arbitrary"),
        name="rg_scan_%s_%s" % ("bwd" if reverse else "fwd", mode),
    )(*args)


def _mm_res_ln_body(a_ref, w_ref, bias_ref, x_ref, m_ref, g_ref, b_ref, o_ref, *, gate_row, alpha, chunk):
    for r0 in range(0, a_ref.shape[0], chunk):
        rows = slice(r0, r0 + chunk)
        y = _dot(a_ref[rows, :].astype(_BF16), w_ref[...]) + bias_ref[...]
        v = alpha * x_ref[rows, :] + m_ref[gate_row:gate_row + 1, :] * y
        o_ref[rows, :] = _layer_norm(v, g_ref[...], b_ref[...])


def _mm_res_ln(a, w, bias, x, mods, layer, gate_row, ln_g, ln_b, alpha, name):
    bsz, s, d = x.shape
    k = a.shape[-1]
    tm = 512
    return pl.pallas_call(
        functools.partial(_mm_res_ln_body, gate_row=gate_row, alpha=alpha, chunk=256),
        grid=(bsz, s // tm),
        in_specs=[pl.BlockSpec((None, tm, k), lambda b, i: (b, i, 0)),
                  pl.BlockSpec((k, d), lambda b, i: (0, 0)),
                  pl.BlockSpec((1, d), lambda b, i: (0, 0)),
                  pl.BlockSpec((None, tm, d), lambda b, i: (b, i, 0)),
                  _mod_spec(layer, d),
                  _vec_spec(2 * layer, d), _vec_spec(2 * layer, d)],
        out_specs=pl.BlockSpec((None, tm, d), lambda b, i: (b, i, 0)),
        out_shape=jax.ShapeDtypeStruct((bsz, s, d), _F32),
        compiler_params=_params("parallel", "parallel"),
        name=name,
    )(a, w, bias, x, mods, ln_g, ln_b)


def _ffn_body(x_ref, m_ref, w1_ref, w3_ref, w2_ref, g_ref, b_ref, o_ref, h_scr, *, nf, alpha, edge_rows):
    f = pl.program_id(2)
    tm = x_ref.shape[0]

    def step(first, last, chunk):
        for r0 in range(0, tm, chunk):
            rows = slice(r0, r0 + chunk)
            if first:
                h_scr[rows, :] = (x_ref[rows, :] * (1.0 + m_ref[4:5, :]) + m_ref[3:4, :]).astype(_BF16)
            h = h_scr[rows, :]
            g = _dot(h, w1_ref[...])
            u = _dot(h, w3_ref[...])
            act = (g * jax.nn.sigmoid(g) * u).astype(_BF16)
            if first:
                o_ref[rows, :] = _dot(act, w2_ref[...])
            else:
                o_ref[rows, :] += _dot(act, w2_ref[...])
            if last:
                v = alpha * x_ref[rows, :] + m_ref[5:6, :] * o_ref[rows, :]
                o_ref[rows, :] = _layer_norm(v, g_ref[...], b_ref[...])

    @pl.when(f == 0)
    def _():
        step(True, False, edge_rows)

    @pl.when(jnp.logical_and(f > 0, f < nf - 1))
    def _():
        step(False, False, tm)

    @pl.when(f == nf - 1)
    def _():
        step(False, True, edge_rows)


def _ffn(x, mods, layer, w1, w3, w2, ln_g, ln_b, alpha):
    bsz, s, d = x.shape
    dff = w1.shape[2]
    tm, tf = 1024, 512
    nf = dff // tf
    xspec = pl.BlockSpec((None, tm, d), lambda b, i, f: (b, i, 0))
    return pl.pallas_call(
        functools.partial(_ffn_body, nf=nf, alpha=alpha, edge_rows=256),
        grid=(bsz, s // tm, nf),
        in_specs=[xspec,
                  _mod_spec(layer, d),
                  pl.BlockSpec((None, d, tf), lambda b, i, f: (layer, 0, f)),
                  pl.BlockSpec((None, d, tf), lambda b, i, f: (layer, 0, f)),
                  pl.BlockSpec((None, tf, d), lambda b, i, f: (layer, f, 0)),
                  _vec_spec(2 * layer + 1, d), _vec_spec(2 * layer + 1, d)],
        out_specs=xspec,
        out_shape=jax.ShapeDtypeStruct((bsz, s, d), _F32),
        scratch_shapes=[pltpu.VMEM((tm, d), _BF16)],
        compiler_params=_params("parallel", "parallel", "arbitrary", vmem=_VMEM_LIMIT_LARGE),
        name="ffn",
    )(x, mods, w1, w3, w2, ln_g, ln_b)


def _pool_body(x_ref, xp_ref, xn_ref, m_ref, w_ref, pb_ref, ps_ref, g_ref, b_ref, o_ref, hext, y_scr,
               *, tm, nchunks, seq, alpha):
    i = pl.program_id(1)
    halo = _V7X_SUBLANES
    scale = 1.0 + m_ref[1:2, :]
    shift = m_ref[0:1, :]
    hext[0:halo, :] = jnp.where(i == 0, 0.0, xp_ref[...] * scale + shift)
    hext[halo:halo + tm, :] = x_ref[...] * scale + shift
    hext[halo + tm:2 * halo + tm, :] = jnp.where(i == nchunks - 1, 0.0, xn_ref[...] * scale + shift)
    t = i * tm + lax.broadcasted_iota(jnp.int32, (tm, 1), 0)
    gd = x_ref.shape[1] // len(POOL_WINDOWS)
    for g, win in enumerate(POOL_WINDOWS):
        cols = slice(g * gd, (g + 1) * gd)
        lo = jnp.maximum(t - win // 2, 0)
        hi = jnp.minimum(t - win // 2 + win, seq)
        inv = 1.0 / (hi - lo).astype(_F32)
        acc = hext[pl.ds(halo - win // 2, tm), cols]
        for j in range(1, win):
            acc = acc + hext[pl.ds(halo - win // 2 + j, tm), cols]
        p = acc * inv - hext[halo:halo + tm, cols]
        y = _dot(p.astype(_BF16), w_ref[g]) + pb_ref[:, cols]
        y_scr[:, cols] = y * ps_ref[:, cols]
    v = alpha * x_ref[...] + m_ref[2:3, :] * y_scr[...]
    o_ref[...] = _layer_norm(v, g_ref[...], b_ref[...])


def _pool_layer(x, mods, layer, pool_w, pool_b, pool_scale, ln_g, ln_b, alpha):
    bsz, s, d = x.shape
    tm = 512
    nchunks = s // tm
    halo = _V7X_SUBLANES
    row = pl.BlockSpec((1, d), lambda b, i: (0, 0))
    xspec = pl.BlockSpec((None, tm, d), lambda b, i: (b, i, 0))
    return pl.pallas_call(
        functools.partial(_pool_body, tm=tm, nchunks=nchunks, seq=s, alpha=alpha),
        grid=(bsz, nchunks),
        in_specs=[xspec,
                  pl.BlockSpec((None, halo, d), lambda b, i: (b, jnp.maximum(i * (tm // halo) - 1, 0), 0)),
                  pl.BlockSpec((None, halo, d),
                               lambda b, i: (b, jnp.minimum((i + 1) * (tm // halo), s // halo - 1), 0)),
                  _mod_spec(layer, d),
                  pl.BlockSpec(pool_w.shape, lambda b, i: (0, 0, 0)),
                  row, row,
                  _vec_spec(2 * layer, d), _vec_spec(2 * layer, d)],
        out_specs=xspec,
        out_shape=jax.ShapeDtypeStruct((bsz, s, d), _F32),
        scratch_shapes=[pltpu.VMEM((tm + 2 * halo, d), _F32), pltpu.VMEM((tm, d), _F32)],
        compiler_params=_params("parallel", "parallel"),
        name="pool_mixer",
    )(x, x, x, mods, pool_w, pool_b, pool_scale, ln_g, ln_b)


def _conf_in_body(x_ref, m_ref, wa_ref, wb_ref, ba_ref, bb_ref, o_ref, h_scr, *, first_chunk):
    tm = x_ref.shape[0]

    def step(first, chunk):
        for r0 in range(0, tm, chunk):
            rows = slice(r0, r0 + chunk)
            if first:
                h_scr[rows, :] = (x_ref[rows, :] * (1.0 + m_ref[1:2, :]) + m_ref[0:1, :]).astype(_BF16)
            h = h_scr[rows, :]
            ua = _dot(h, wa_ref[...]) + ba_ref[...]
            ub = _dot(h, wb_ref[...]) + bb_ref[...]
            o_ref[rows, :] = ua * jax.nn.sigmoid(ub)

    @pl.when(pl.program_id(2) == 0)
    def _():
        step(True, first_chunk)

    @pl.when(pl.program_id(2) > 0)
    def _():
        step(False, tm)


def _conf_in(x, mods, layer, w1, b1):
    bsz, s, d = x.shape
    tm, tn = 1024, 1024
    nj = d // tn
    return pl.pallas_call(
        functools.partial(_conf_in_body, first_chunk=256),
        grid=(bsz, s // tm, nj),
        in_specs=[pl.BlockSpec((None, tm, d), lambda b, i, j: (b, i, 0)),
                  _mod_spec(layer, d),
                  pl.BlockSpec((d, tn), lambda b, i, j: (0, j)),
                  pl.BlockSpec((d, tn), lambda b, i, j: (0, j + nj)),
                  pl.BlockSpec((1, tn), lambda b, i, j: (0, j)),
                  pl.BlockSpec((1, tn), lambda b, i, j: (0, j + nj))],
        out_specs=pl.BlockSpec((None, tm, tn), lambda b, i, j: (b, i, j)),
        out_shape=jax.ShapeDtypeStruct((bsz, s, d), _F32),
        scratch_shapes=[pltpu.VMEM((tm, d), _BF16)],
        compiler_params=_params("parallel", "parallel", "arbitrary", vmem=_VMEM_LIMIT_LARGE),
        name="conf_in",
    )(x, mods, w1, w1, b1, b1)


_CONF_HALO = 16


def _conf_out_body(u_ref, up_ref, un_ref, x_ref, m_ref, dw_ref, cg_ref, cb_ref, w2_ref, b2_ref,
                   g_ref, b_ref, o_ref, uslab, pslab, nslab, uperm, conv_scr, *, tm, nchunks, alpha):
    i = pl.program_id(1)
    halo = _CONF_HALO
    lanes = _V7X_LANES
    d = x_ref.shape[1]
    ncol = d // lanes
    seg_len = tm // _NSEG
    pad = (CONF_KERNEL - 1) // 2
    first = i == 0
    last = i == nchunks - 1
    for cb_ in range(ncol):
        cols = slice(cb_ * lanes, (cb_ + 1) * lanes)
        uslab[cb_] = u_ref[:, cols]
        pslab[cb_] = jnp.where(first, 0.0, up_ref[:, cols])
        nslab[cb_] = jnp.where(last, 0.0, un_ref[:, cols])

    def lane_block(cb_, carry):
        perm = _to_segment_major(uslab[cb_])
        uperm[pad * _NSEG:pad * _NSEG + tm, :] = perm
        for t in range(-pad, 0):
            src_rows = slice((seg_len + t) * _NSEG, (seg_len + t + 1) * _NSEG)
            uperm[(t + pad) * _NSEG:(t + pad + 1) * _NSEG, :] = _neighbour_segment_tile(
                perm[src_rows, :], pslab[cb_, halo + t:halo + t + 1, :], toward_next=False)
        for t in range(seg_len, seg_len + pad):
            src_rows = slice((t - seg_len) * _NSEG, (t - seg_len + 1) * _NSEG)
            uperm[(t + pad) * _NSEG:(t + pad + 1) * _NSEG, :] = _neighbour_segment_tile(
                perm[src_rows, :], nslab[cb_, t - seg_len:t - seg_len + 1, :], toward_next=True)
        w = [jnp.broadcast_to(dw_ref[cb_, k:k + 1, :], (_NSEG, lanes)) for k in range(CONF_KERNEL)]
        bias = jnp.broadcast_to(dw_ref[cb_, CONF_KERNEL:CONF_KERNEL + 1, :], (_NSEG, lanes))
        for j in range(0, seg_len, 2):
            acc0 = bias
            acc1 = bias
            for q in range(j, j + CONF_KERNEL + 1):
                tap = uperm[q * _NSEG:(q + 1) * _NSEG, :]
                if q - j < CONF_KERNEL:
                    acc0 = acc0 + w[q - j] * tap
                if q - j >= 1:
                    acc1 = acc1 + w[q - j - 1] * tap
            conv_scr[cb_, j * _NSEG:(j + 1) * _NSEG, :] = acc0
            conv_scr[cb_, (j + 1) * _NSEG:(j + 2) * _NSEG, :] = acc1
        return carry

    lax.fori_loop(0, ncol, lane_block, 0)

    cv = jnp.concatenate([conv_scr[cb_] for cb_ in range(ncol)], axis=-1)
    u2 = _layer_norm(cv, cg_ref[...], cb_ref[...])
    u2 = u2 * jax.nn.sigmoid(u2)
    y = _from_segment_major(_dot(u2.astype(_BF16), w2_ref[...]) + b2_ref[...])
    v = alpha * x_ref[...] + m_ref[2:3, :] * y
    o_ref[...] = _layer_norm(v, g_ref[...], b_ref[...])


def _conf_out(u, x, mods, layer, dw_blocks, cg, cb, w2, b2, ln_g, ln_b, alpha):
    bsz, s, d = x.shape
    tm = 256
    nchunks = s // tm
    halo = _CONF_HALO
    ncol = d // _V7X_LANES
    row = pl.BlockSpec((1, d), lambda b, i: (0, 0))
    xspec = pl.BlockSpec((None, tm, d), lambda b, i: (b, i, 0))
    return pl.pallas_call(
        functools.partial(_conf_out_body, tm=tm, nchunks=nchunks, alpha=alpha),
        grid=(bsz, nchunks),
        in_specs=[xspec,
                  pl.BlockSpec((None, halo, d), lambda b, i: (b, jnp.maximum(i * (tm // halo) - 1, 0), 0)),
                  pl.BlockSpec((None, halo, d),
                               lambda b, i: (b, jnp.minimum((i + 1) * (tm // halo), s // halo - 1), 0)),
                  xspec,
                  _mod_spec(layer, d),
                  pl.BlockSpec(dw_blocks.shape, lambda b, i: (0, 0, 0)),
                  row, row,
                  pl.BlockSpec((d, d), lambda b, i: (0, 0)),
                  row,
                  _vec_spec(2 * layer, d), _vec_spec(2 * layer, d)],
        out_specs=xspec,
        out_shape=jax.ShapeDtypeStruct((bsz, s, d), _F32),
        scratch_shapes=[pltpu.VMEM((ncol, tm, _V7X_LANES), _F32),
                        pltpu.VMEM((ncol, halo, _V7X_LANES), _F32),
                        pltpu.VMEM((ncol, halo, _V7X_LANES), _F32),
                        pltpu.VMEM((tm + (CONF_KERNEL - 1) * _NSEG, _V7X_LANES), _F32),
                        pltpu.VMEM((ncol, tm, _V7X_LANES), _F32)],
        compiler_params=_params("parallel", "parallel"),
        name="conf_out",
    )(u, u, u, x, mods, dw_blocks, cg, cb, w2, b2, ln_g, ln_b)


def _ft_in_body(x_ref, m_ref, w_ref, zr_ref, zi_ref, *, groups):
    h = (x_ref[...] * (1.0 + m_ref[1:2, :]) + m_ref[0:1, :]).astype(_BF16)
    gd = x_ref.shape[1] // groups
    for g in range(groups):
        cols = slice(g * gd, (g + 1) * gd)
        z = _dot(h[:, cols], w_ref[...])
        zr_ref[:, cols] = z[:, :gd].astype(_BF16)
        zi_ref[:, cols] = z[:, gd:].astype(_BF16)


def _ft_in(x, mods, layer, w_dft):
    bsz, s, d = x.shape
    tm = 512
    xspec = pl.BlockSpec((None, tm, d), lambda b, i: (b, i, 0))
    return pl.pallas_call(
        functools.partial(_ft_in_body, groups=FT_GROUPS),
        grid=(bsz, s // tm),
        in_specs=[xspec, _mod_spec(layer, d),
                  pl.BlockSpec(w_dft.shape, lambda b, i: (0, 0))],
        out_specs=[xspec, xspec],
        out_shape=[jax.ShapeDtypeStruct((bsz, s, d), _BF16)] * 2,
        compiler_params=_params("parallel", "parallel"),
        name="ft_channel_dft",
    )(x, mods, w_dft)


def _swap_major(v, n_major):
    rows, c = v.shape
    n_minor = rows // n_major
    return pltpu.einshape("abc->bac", v.reshape(n_major, n_minor, c)).reshape(rows, c)


def _ft_seq_body(zr_ref, zi_ref, ma_ref, w2_ref, o_ref, pr_scr, pi_scr, yr_scr, yi_scr, f_scr, *, n1, n2):
    pr_scr[...] = _swap_major(zr_ref[...], n1)
    pi_scr[...] = _swap_major(zi_ref[...], n1)
    for s2 in range(n2):
        rows = slice(s2 * n1, (s2 + 1) * n1)
        zc = jnp.concatenate([pr_scr[rows, :], pi_scr[rows, :]], axis=0)
        y = _dot(ma_ref[s2], zc).astype(_BF16)
        yr_scr[rows, :] = y[:n1]
        yi_scr[rows, :] = y[n1:]
    pr_scr[...] = _swap_major(yr_scr[...], n2)
    pi_scr[...] = _swap_major(yi_scr[...], n2)
    for k1 in range(n1):
        rows = slice(k1 * n2, (k1 + 1) * n2)
        yc = jnp.concatenate([pr_scr[rows, :], pi_scr[rows, :]], axis=0)
        f_scr[rows, :] = _dot(w2_ref[...], yc)
    o_ref[...] = _swap_major(f_scr[...], n1)


def _ft_seq(zr, zi, ma, w2cat, n1, n2):
    bsz, s, d = zr.shape
    tc = _V7X_LANES
    spec = pl.BlockSpec((None, s, tc), lambda b, j: (b, 0, j))
    return pl.pallas_call(
        functools.partial(_ft_seq_body, n1=n1, n2=n2),
        grid=(bsz, d // tc),
        in_specs=[spec, spec,
                  pl.BlockSpec(ma.shape, lambda b, j: (0, 0, 0)),
                  pl.BlockSpec(w2cat.shape, lambda b, j: (0, 0))],
        out_specs=spec,
        out_shape=jax.ShapeDtypeStruct((bsz, s, d), _F32),
        scratch_shapes=[pltpu.VMEM((s, tc), _BF16)] * 4 + [pltpu.VMEM((s, tc), _F32)],
        compiler_params=_params("parallel", "parallel"),
        name="ft_sequence_dft",
    )(zr, zi, ma, w2cat)


def _ft_tables(s, gd):
    n2 = 128
    n1 = s // n2
    p = np.arange(gd)
    ang = 2.0 * np.pi * ((p[:, None] * p[None, :]) % gd) / gd
    scale = 1.0 / np.sqrt(float(s) * gd)
    w_dft = np.concatenate([np.cos(ang), -np.sin(ang)], axis=1) * scale
    k1 = np.arange(n1)
    s1 = np.arange(n1)
    s2 = np.arange(n2)
    pos = n2 * s1[None, None, :] + s2[:, None, None]
    ang = 2.0 * np.pi * ((k1[None, :, None] * pos) % s) / s
    er, ei = np.cos(ang), np.sin(ang)
    ma = np.concatenate([np.concatenate([er, ei], axis=2),
                         np.concatenate([-ei, er], axis=2)], axis=1)
    k2 = np.arange(n2)
    ang = 2.0 * np.pi * ((k2[:, None] * s2[None, :]) % n2) / n2
    w2cat = np.concatenate([np.cos(ang), np.sin(ang)], axis=1)
    as_bf16 = lambda a: jnp.asarray(a.astype(np.float32)).astype(_BF16)
    return as_bf16(w_dft), as_bf16(ma), as_bf16(w2cat), n1, n2


def _pad_blocks_last(a):
    lead = a.shape[:-1]
    blk = a.shape[-1] // RG_BLOCKS
    a = a.reshape(lead + (RG_BLOCKS, blk))
    a = jnp.pad(a, [(0, 0)] * len(lead) + [(0, 0), (0, _RG_BLOCK_PAD - blk)])
    return a.reshape(lead + (RG_BLOCKS * _RG_BLOCK_PAD,))


def _pair_block_diag(w):
    blk = w.shape[-1]
    w = jnp.pad(w, [(0, 0), (0, _RG_BLOCK_PAD - blk), (0, _RG_BLOCK_PAD - blk)])
    w = w.reshape(RG_BLOCKS // 2, 2, _RG_BLOCK_PAD, _RG_BLOCK_PAD)
    zero = jnp.zeros_like(w[:, 0])
    top = jnp.concatenate([w[:, 0], zero], axis=-1)
    bottom = jnp.concatenate([zero, w[:, 1]], axis=-1)
    return jnp.concatenate([top, bottom], axis=1)


def _pos_tables(rows, cols, dim):
    quarter = dim // 4
    omega = 1.0 / (POS_THETA ** (jnp.arange(quarter, dtype=_F32) / quarter))
    ar = jnp.arange(rows, dtype=_F32)[:, None] * omega[None]
    ac = jnp.arange(cols, dtype=_F32)[:, None] * omega[None]
    er = jnp.concatenate([jnp.sin(ar), jnp.cos(ar)], axis=-1)
    ec = jnp.concatenate([jnp.sin(ac), jnp.cos(ac)], axis=-1)
    return er, ec


def _rglru_layer(x, ctx, mods, layer, bsz, w_gate, w_x, conv_w, conv_b, wr, br, wi, bi, lam, w_out,
                 ln_g, ln_b, alpha):
    s, d = x.shape[1], x.shape[2]
    w_x_p = _pad_blocks_last(w_x.astype(_BF16))
    w_gate_p = _pad_blocks_last(w_gate.astype(_BF16))
    conv_w_p = _pad_blocks_last(conv_w)
    conv_b_p = _pad_blocks_last(conv_b)[None]
    blk = w_out.shape[0] // RG_BLOCKS
    w_out_p = jnp.pad(w_out.reshape(RG_BLOCKS, blk, d), [(0, 0), (0, _RG_BLOCK_PAD - blk), (0, 0)])
    w_out_p = w_out_p.reshape(RG_BLOCKS * _RG_BLOCK_PAD, d).astype(_BF16)
    npairs = RG_BLOCKS // 2
    er_ec = _pos_tables(s // GRID_W, GRID_W, d)
    x0, xw, gate = _rg_in(x, mods, layer, lambda b: b, w_x_p, w_gate_p, er_ec)
    xwc = _rg_in(ctx, mods, layer, lambda b: bsz, w_x_p, None, None)
    zeros_state = jnp.zeros((bsz, 1, xw.shape[-1]), _F32)
    hf = None
    out = None
    for direction in range(2):
        reverse = direction == 1
        w_gates = (0.5 * jnp.concatenate([_pair_block_diag(wr[direction]), _pair_block_diag(wi[direction])],
                                         axis=-1)).astype(_BF16)
        b_gates = 0.5 * jnp.concatenate([_pad_blocks_last(br[direction]).reshape(npairs, 1, _RG_PAIR),
                                         _pad_blocks_last(bi[direction]).reshape(npairs, 1, _RG_PAIR)], axis=-1)
        lam_p = _pad_blocks_last(lam[direction])[None]
        _, h0 = _rg_scan(xwc, conv_w_p, conv_b_p, w_gates, b_gates, lam_p, zeros_state, reverse)
        if not reverse:
            hf, _, xb = _rg_scan(xw, conv_w_p, conv_b_p, w_gates, b_gates, lam_p, h0, reverse, "first")
        else:
            out, _ = _rg_scan(xb, None, None, w_gates, b_gates, lam_p, h0, reverse, "final", hf, gate)
    zero_bias = jnp.zeros((1, d), _F32)
    return _mm_res_ln(out, w_out_p, zero_bias, x0, mods, layer, 2, ln_g, ln_b, alpha, "rg_out")


def kernel(x, c, ctx, c_ctx, mod_w, mod_b, ln_g, ln_b, ffn_w1, ffn_w3, ffn_w2, rg_w_gate, rg_w_x, rg_conv_w, rg_conv_b, rg_wr, rg_br, rg_wi, rg_bi, rg_lam, rg_w_out, pool_w, pool_b, pool_scale, cv_w1, cv_b1, cv_dw, cv_dwb, cv_ln_g, cv_ln_b, cv_w2, cv_b2, ft_w, ft_b):
    bsz, s, d = x.shape
    depth = mod_w.shape[0]
    alpha = float((2 * depth) ** 0.25)
    assert depth == N_MIXERS and bsz < _MOD_ROWS, "one layer per mixer kind; the context feeds layer 0 only"

    cond = jnp.zeros((_MOD_ROWS, d), _F32).at[:bsz].set(c).at[bsz].set(c_ctx)
    mods = _cond_vectors(cond, mod_w, mod_b).reshape(depth, _MOD_ROWS, 6, d)
    ln_g3 = ln_g.reshape(depth * 2, 1, d)
    ln_b3 = ln_b.reshape(depth * 2, 1, d)
    w1_all = ffn_w1.astype(_BF16)
    w3_all = ffn_w3.astype(_BF16)
    w2_all = ffn_w2.astype(_BF16)

    def row(v):
        return v.reshape(1, -1)

    def ffn(xc, layer):
        return _ffn(xc, mods, layer, w1_all, w3_all, w2_all, ln_g3, ln_b3, alpha)

    xc = _rglru_layer(x, ctx, mods, 0, bsz, rg_w_gate[0], rg_w_x[0], rg_conv_w[0], rg_conv_b[0],
                      rg_wr[0], rg_br[0], rg_wi[0], rg_bi[0], rg_lam[0], rg_w_out[0], ln_g3, ln_b3, alpha)
    xc = ffn(xc, 0)

    xc = _pool_layer(xc, mods, 1, pool_w[0].astype(_BF16), row(pool_b[0]), row(pool_scale[0]),
                     ln_g3, ln_b3, alpha)
    xc = ffn(xc, 1)

    u = _conf_in(xc, mods, 2, cv_w1[0].astype(_BF16), row(cv_b1[0]))
    ncol = d // _V7X_LANES
    dw_blocks = jnp.concatenate([cv_dw[0], cv_dwb[0][None]], axis=0)
    dw_blocks = dw_blocks.reshape(CONF_KERNEL + 1, ncol, _V7X_LANES).transpose(1, 0, 2)
    xc = _conf_out(u, xc, mods, 2, dw_blocks, row(cv_ln_g[0]), row(cv_ln_b[0]),
                   cv_w2[0].astype(_BF16), row(cv_b2[0]), ln_g3, ln_b3, alpha)
    xc = ffn(xc, 2)

    w_dft, ma, w2cat, n1, n2 = _ft_tables(s, d // FT_GROUPS)
    zr, zi = _ft_in(xc, mods, 3, w_dft)
    f = _ft_seq(zr, zi, ma, w2cat, n1, n2)
    xc = _mm_res_ln(f, ft_w[0].astype(_BF16), row(ft_b[0]), xc, mods, 3, 2, ln_g3, ln_b3, alpha, "ft_out")
    xc = ffn(xc, 3)
    return xc
```
